```python
import math
import jax, jax.numpy as jnp
from jax import lax
import numpy as np


D_MODEL = 1024
BATCH = 4
SEQ = 8192
DEPTH = 2

SSD_EXPAND = 2
SSD_INNER = SSD_EXPAND * D_MODEL
SSD_HEAD_DIM = 64
SSD_HEADS = SSD_INNER // SSD_HEAD_DIM
SSD_GROUPS = 4
SSD_STATE = 128
SSD_CONV = 5
SSD_CHUNK = 128
SSD_CONV_CH = SSD_INNER + 2 * SSD_GROUPS * SSD_STATE

DIFF_HEAD_DIM = 64
DIFF_HEADS = D_MODEL // (2 * DIFF_HEAD_DIM)
DIFF_WIDTH = DIFF_HEADS * 2 * DIFF_HEAD_DIM
Q_BLOCK = 128
ROPE_THETA = 10000.0

MEM_TOKENS = 256
CROSS_HEADS = 4
CROSS_HEAD_DIM = D_MODEL // CROSS_HEADS
CROSS_WIDTH = CROSS_HEADS * CROSS_HEAD_DIM

N_BRANCH = 3
IN_COLS = (SSD_INNER + SSD_CONV_CH + 2 * SSD_HEADS + 4 * DIFF_WIDTH
           + 2 * CROSS_WIDTH + N_BRANCH * D_MODEL)
DEEPNORM_ALPHA = (2 * DEPTH) ** 0.25
DEEPNORM_BETA = (8 * DEPTH) ** -0.25
NORM_EPS = 1e-5

kernel_name = 'hybrid_ssd_diffattn_memxattn_deepnorm'


def _in_splits():
    sizes = [SSD_INNER, SSD_CONV_CH, 2 * SSD_HEADS,
             DIFF_WIDTH, DIFF_WIDTH, DIFF_WIDTH, DIFF_WIDTH,
             CROSS_WIDTH, CROSS_WIDTH]
    return [int(v) for v in np.cumsum(sizes)]


def _rmsnorm(t, w):
    tf = t.astype(jnp.float32)
    tf = tf * lax.rsqrt(jnp.mean(tf * tf, axis=-1, keepdims=True) + NORM_EPS)
    return tf * w.astype(jnp.float32)


def _layernorm(t, g, b):
    tf = t.astype(jnp.float32)
    mu = jnp.mean(tf, axis=-1, keepdims=True)
    var = jnp.mean(jnp.square(tf - mu), axis=-1, keepdims=True)
    return (tf - mu) * lax.rsqrt(var + NORM_EPS) * g.astype(jnp.float32) + b.astype(jnp.float32)


def _rope_tables(positions):
    inv = 1.0 / (ROPE_THETA ** (jnp.arange(0, DIFF_HEAD_DIM, 2, dtype=jnp.float32) / DIFF_HEAD_DIM))
    ang = positions.astype(jnp.float32)[..., None] * inv
    return jnp.cos(ang), jnp.sin(ang)


def _apply_rope(t, cos, sin):
    half = DIFF_HEAD_DIM // 2
    tf = t.astype(jnp.float32)
    t1, t2 = tf[..., :half], tf[..., half:]
    c = cos[:, :, None, None, :]
    s = sin[:, :, None, None, :]
    return jnp.concatenate([t1 * c - t2 * s, t1 * s + t2 * c], axis=-1)


def _centred_dwconv(t, w, b):
    pad = SSD_CONV // 2
    out = lax.conv_general_dilated(
        t, w[:, None, :].astype(t.dtype), window_strides=(1,),
        padding=[(pad, pad)], dimension_numbers=('NWC', 'WIO', 'NWC'),
        feature_group_count=t.shape[-1])
    return out + b.astype(t.dtype)


def _ssd_scan(xs, a, bm, cm):
    Bsz, L, H, P = xs.shape
    G, N = bm.shape[2], bm.shape[3]
    R = H // G
    Q = SSD_CHUNK
    C = L // Q
    xs = xs.astype(jnp.float32).reshape(Bsz, C, Q, G, R, P)
    bm = bm.astype(jnp.float32).reshape(Bsz, C, Q, G, N)
    cm = cm.astype(jnp.float32).reshape(Bsz, C, Q, G, N)
    a = a.astype(jnp.float32).reshape(Bsz, C, Q, G, R).transpose(0, 3, 4, 1, 2)
    a_cum = jnp.cumsum(a, axis=-1)
    lower = jnp.tril(jnp.ones((Q, Q), dtype=bool))
    seg = a_cum[..., :, None] - a_cum[..., None, :]
    decay = jnp.exp(jnp.where(lower, seg, -jnp.inf))
    y_diag = jnp.einsum('bcqgn,bcsgn,bgrcqs,bcsgrp->bcqgrp', cm, bm, decay, xs)
    decay_states = jnp.exp(a_cum[..., -1:] - a_cum)
    states = jnp.einsum('bcqgn,bgrcq,bcqgrp->bcgrpn', bm, decay_states, xs)
    chunk_decay = jnp.moveaxis(jnp.exp(a_cum[..., -1]), -1, 0)
    def step(h, inp):
        s_c, d_c = inp
        return h * d_c[..., None, None] + s_c, h
    h0 = jnp.zeros((Bsz, G, R, P, N), jnp.float32)
    _, prev = lax.scan(step, h0, (jnp.moveaxis(states, 1, 0), chunk_decay))
    prev = jnp.moveaxis(prev, 0, 1)
    y_off = jnp.einsum('bcqgn,bcgrpn,bgrcq->bcqgrp', cm, prev, jnp.exp(a_cum))
    return (y_diag + y_off).reshape(Bsz, L, H, P)


def _ssd_branch(z, xbc, dt_raw, conv_w, conv_b, dt_bias, a_log, d_skip, norm_w):
    Bsz, L, _ = z.shape
    gn = SSD_GROUPS * SSD_STATE
    xbc = jax.nn.silu(_centred_dwconv(xbc, conv_w, conv_b))
    xs = xbc[..., :SSD_INNER].reshape(Bsz, L, SSD_HEADS, SSD_HEAD_DIM).astype(jnp.float32)
    bm = xbc[..., SSD_INNER:SSD_INNER + gn].reshape(Bsz, L, SSD_GROUPS, SSD_STATE)
    cm = xbc[..., SSD_INNER + gn:].reshape(Bsz, L, SSD_GROUPS, SSD_STATE)
    dt = jax.nn.softplus(dt_raw.reshape(Bsz, L, 2, SSD_HEADS).astype(jnp.float32)
                         + dt_bias.astype(jnp.float32))
    a = -jnp.exp(a_log.astype(jnp.float32))
    flip = lambda t: jnp.flip(t, axis=1)
    y_fwd = _ssd_scan(xs * dt[:, :, 0, :, None], dt[:, :, 0] * a[0], bm, cm)
    y_bwd = flip(_ssd_scan(flip(xs * dt[:, :, 1, :, None]), flip(dt[:, :, 1] * a[1]),
                           flip(bm), flip(cm)))
    y = y_fwd + y_bwd + d_skip.astype(jnp.float32)[:, None] * xs
    y = y.reshape(Bsz, L, SSD_INNER) * jax.nn.silu(z.astype(jnp.float32))
    return _rmsnorm(y, norm_w)


def _diff_attention(q, k, v, lam):
    Bsz, S = q.shape[0], q.shape[1]
    nb = S // Q_BLOCK
    qb = jnp.moveaxis(q.reshape(Bsz, nb, Q_BLOCK, DIFF_HEADS, 2, DIFF_HEAD_DIM), 1, 0)
    vf = v.astype(jnp.float32)
    scale = DIFF_HEAD_DIM ** -0.5
    def block(qblk):
        s = jnp.einsum('bqhcd,bkhcd->bhcqk', qblk, k) * scale
        p = jax.nn.softmax(s.astype(jnp.float32), axis=-1)
        w = p[:, :, 0] - lam * p[:, :, 1]
        return jnp.einsum('bhqk,bkhe->bqhe', w, vf)
    o = lax.map(block, qb)
    return jnp.moveaxis(o, 0, 1).reshape(Bsz, S, DIFF_HEADS, 2 * DIFF_HEAD_DIM)


def _memory_cross_attention(q, mem_kv):
    mk, mv = mem_kv[:, :, 0], mem_kv[:, :, 1]
    s = jnp.einsum('bshd,bmhd->bhsm', q, mk).astype(jnp.float32) * (CROSS_HEAD_DIM ** -0.5)
    p = jax.nn.softmax(s, axis=-1)
    return jnp.einsum('bhsm,bmhd->bshd', p, mv.astype(jnp.float32))


def setup_inputs(seed: int = 0) -> dict:
    key = jax.random.key(seed)
    ks = jax.random.split(key, 20)
    f32 = jnp.float32
    def nrm(k, shape, scale):
        return jax.random.normal(k, shape, f32) * scale
    x = nrm(ks[0], (BATCH, SEQ, D_MODEL), 1.0)
    mem = nrm(ks[1], (BATCH, MEM_TOKENS, D_MODEL), 1.0)
    offsets = jax.random.randint(ks[2], (BATCH, 1), 0, 4096, dtype=jnp.int32)
    positions = offsets + jnp.arange(SEQ, dtype=jnp.int32)[None, :]
    w_in = nrm(ks[3], (DEPTH, D_MODEL, IN_COLS), D_MODEL ** -0.5)
    conv_w = nrm(ks[4], (DEPTH, SSD_CONV, SSD_CONV_CH), SSD_CONV ** -0.5)
    conv_b = nrm(ks[5], (DEPTH, SSD_CONV_CH), 0.01)
    dt0 = jnp.exp(jax.random.uniform(ks[6], (DEPTH, 2, SSD_HEADS), f32,
                                     math.log(1e-3), math.log(1e-1)))
    dt_bias = dt0 + jnp.log(-jnp.expm1(-dt0))
    a_log = jnp.log(jax.random.uniform(ks[7], (DEPTH, 2, SSD_HEADS), f32, 1.0, 16.0))
    d_skip = 1.0 + nrm(ks[8], (DEPTH, SSD_HEADS), 0.01)
    ssd_norm_w = 1.0 + nrm(ks[9], (DEPTH, SSD_INNER), 0.01)
    diff_lam = nrm(ks[10], (DEPTH, 4, DIFF_HEAD_DIM), 0.1)
    diff_norm_w = 1.0 + nrm(ks[11], (DEPTH, 2 * DIFF_HEAD_DIM), 0.01)
    w_mem_kv = nrm(ks[12], (DEPTH, D_MODEL, 2 * CROSS_WIDTH), D_MODEL ** -0.5)
    w_br_ssd = nrm(ks[13], (DEPTH, SSD_INNER, D_MODEL), SSD_INNER ** -0.5 * DEEPNORM_BETA)
    w_br_diff = nrm(ks[14], (DEPTH, DIFF_WIDTH, D_MODEL), DIFF_WIDTH ** -0.5 * DEEPNORM_BETA)
    w_br_cross = nrm(ks[15], (DEPTH, CROSS_WIDTH, D_MODEL), CROSS_WIDTH ** -0.5 * DEEPNORM_BETA)
    gate_b = nrm(ks[16], (DEPTH, N_BRANCH, D_MODEL), 0.01)
    w_out = nrm(ks[17], (DEPTH, D_MODEL, D_MODEL), D_MODEL ** -0.5 * DEEPNORM_BETA)
    ln_g = 1.0 + nrm(ks[18], (DEPTH, D_MODEL), 0.01)
    ln_b = nrm(ks[19], (DEPTH, D_MODEL), 0.01)
    return {'x': x, 'mem': mem, 'positions': positions, 'w_in': w_in,
            'conv_w': conv_w, 'conv_b': conv_b, 'dt_bias': dt_bias, 'a_log': a_log,
            'd_skip': d_skip, 'ssd_norm_w': ssd_norm_w, 'diff_lam': diff_lam,
            'diff_norm_w': diff_norm_w, 'w_mem_kv': w_mem_kv, 'w_br_ssd': w_br_ssd,
            'w_br_diff': w_br_diff, 'w_br_cross': w_br_cross, 'gate_b': gate_b,
            'w_out': w_out, 'ln_g': ln_g, 'ln_b': ln_b}


def reference(x, mem, positions, w_in, conv_w, conv_b, dt_bias, a_log, d_skip,
              ssd_norm_w, diff_lam, diff_norm_w, w_mem_kv, w_br_ssd, w_br_diff,
              w_br_cross, gate_b, w_out, ln_g, ln_b):
    Bsz, S, _ = x.shape
    cos, sin = _rope_tables(positions)
    splits = _in_splits()
    for layer in range(DEPTH):
        lambda_init = 0.8 - 0.6 * math.exp(-0.3 * layer)
        proj = jnp.einsum('bsd,de->bse', x, w_in[layer])
        z, xbc, dt_raw, dq, dk, dv, dg, cq, cg, gl = jnp.split(proj, splits, axis=-1)

        y_ssd = _ssd_branch(z, xbc, dt_raw, conv_w[layer], conv_b[layer], dt_bias[layer],
                            a_log[layer], d_skip[layer], ssd_norm_w[layer])

        lq = diff_lam[layer].astype(jnp.float32)
        lam = (jnp.exp(jnp.sum(lq[0] * lq[1])) - jnp.exp(jnp.sum(lq[2] * lq[3]))
               + lambda_init)
        q = _apply_rope(dq.reshape(Bsz, S, DIFF_HEADS, 2, DIFF_HEAD_DIM), cos, sin)
        k = _apply_rope(dk.reshape(Bsz, S, DIFF_HEADS, 2, DIFF_HEAD_DIM), cos, sin)
        v = dv.reshape(Bsz, S, DIFF_HEADS, 2 * DIFF_HEAD_DIM)
        o = _diff_attention(q, k, v, lam)
        o = _rmsnorm(o, diff_norm_w[layer]) * (1.0 - lambda_init)
        y_diff = o.reshape(Bsz, S, DIFF_WIDTH) * jax.nn.silu(dg.astype(jnp.float32))

        mem_kv = jnp.einsum('bmd,de->bme', mem, w_mem_kv[layer]).reshape(
            Bsz, MEM_TOKENS, 2, CROSS_HEADS, CROSS_HEAD_DIM)
        oc = _memory_cross_attention(cq.reshape(Bsz, S, CROSS_HEADS, CROSS_HEAD_DIM), mem_kv)
        y_cross = oc.reshape(Bsz, S, CROSS_WIDTH) * jax.nn.silu(cg.astype(jnp.float32))

        gates = jax.nn.sigmoid(gl.reshape(Bsz, S, N_BRANCH, D_MODEL).astype(jnp.float32)
                               + gate_b[layer].astype(jnp.float32))
        merged = (gates[:, :, 0] * jnp.einsum('bse,ed->bsd', y_ssd, w_br_ssd[layer])
                  + gates[:, :, 1] * jnp.einsum('bse,ed->bsd', y_diff, w_br_diff[layer])
                  + gates[:, :, 2] * jnp.einsum('bse,ed->bsd', y_cross, w_br_cross[layer]))
        out = jnp.einsum('bsd,de->bse', merged, w_out[layer])

        x = _layernorm(DEEPNORM_ALPHA * x.astype(jnp.float32) + out,
                       ln_g[layer], ln_b[layer]).astype(x.dtype)
    return x
```

```python
import functools
import math

import numpy as np
import jax
import jax.numpy as jnp
from jax import lax
from jax.experimental import pallas as pl
from jax.experimental.pallas import tpu as pltpu

F32 = jnp.float32
BF16 = jnp.bfloat16

D_MODEL = 1024
DEPTH = 2
SSD_INNER = 2048
SSD_HEAD_DIM = 64
SSD_HEADS = 32
SSD_GROUPS = 4
SSD_HEADS_PER_GROUP = SSD_HEADS // SSD_GROUPS
SSD_STATE = 128
SSD_CONV = 5
SSD_CHUNK = 128
SSD_CONV_CH = SSD_INNER + 2 * SSD_GROUPS * SSD_STATE
DIFF_HEAD_DIM = 64
DIFF_HEADS = 8
DIFF_WIDTH = 1024
ROPE_THETA = 10000.0
CROSS_HEADS = 4
CROSS_HEAD_DIM = 256
CROSS_WIDTH = 1024
N_BRANCH = 3
DEEPNORM_ALPHA = (2 * DEPTH) ** 0.25
NORM_EPS = 1e-5

LANES = 128
HALO = 16
VMEM_LIMIT = 56 * 1024 * 1024

_NT = (((1,), (1,)), ((), ()))


def _sigmoid(v):
    return 1.0 / (1.0 + jnp.exp(-v))


def _silu(v):
    return v * _sigmoid(v)


def _cparams(sem):
    return pltpu.CompilerParams(dimension_semantics=sem, vmem_limit_bytes=VMEM_LIMIT)


def _proj_plain_kernel(x_ref, w_ref, o_ref):
    o_ref[...] = jnp.dot(x_ref[...], w_ref[...],
                         preferred_element_type=F32).astype(o_ref.dtype)


def _proj_plain(xb, w, out_dtype, tm, tn):
    T, K = xb.shape
    N = w.shape[1]
    tm, tn = min(tm, T), min(tn, N)
    return pl.pallas_call(
        _proj_plain_kernel,
        grid=(N // tn, T // tm),
        in_specs=[pl.BlockSpec((tm, K), lambda j, i: (i, 0)),
                  pl.BlockSpec((K, tn), lambda j, i: (0, j))],
        out_specs=pl.BlockSpec((tm, tn), lambda j, i: (i, j)),
        out_shape=jax.ShapeDtypeStruct((T, N), out_dtype),
        compiler_params=_cparams(("parallel", "parallel")),
        name="proj_plain",
    )(xb, w)


def _proj_rope_kernel(x_ref, w_ref, cos_ref, sin_ref, o_ref, *, scale):
    acc = jnp.dot(x_ref[...], w_ref[...], preferred_element_type=F32)
    c = cos_ref[...]
    s = sin_ref[...]
    for j in range(acc.shape[1] // LANES):
        t = acc[:, j * LANES:(j + 1) * LANES]
        r = pltpu.roll(t, LANES // 2, axis=1)
        o = t * c + r * s
        if scale != 1.0:
            o = o * scale
        o_ref[:, j * LANES:(j + 1) * LANES] = o.astype(o_ref.dtype)


def _proj_rope(xb, w, cos_t, sin_t, scale, tm, tn):
    T, K = xb.shape
    N = w.shape[1]
    tm, tn = min(tm, T), min(tn, N)
    return pl.pallas_call(
        functools.partial(_proj_rope_kernel, scale=scale),
        grid=(N // tn, T // tm),
        in_specs=[pl.BlockSpec((tm, K), lambda j, i: (i, 0)),
                  pl.BlockSpec((K, tn), lambda j, i: (0, j)),
                  pl.BlockSpec((tm, LANES), lambda j, i: (i, 0)),
                  pl.BlockSpec((tm, LANES), lambda j, i: (i, 0))],
        out_specs=pl.BlockSpec((tm, tn), lambda j, i: (i, j)),
        out_shape=jax.ShapeDtypeStruct((T, N), BF16),
        compiler_params=_cparams(("parallel", "parallel")),
        name="proj_rope",
    )(xb, w, cos_t, sin_t)


def _proj_nt_kernel(wt_ref, x_ref, o_ref):
    o_ref[...] = lax.dot_general(wt_ref[...], x_ref[...], _NT,
                                 preferred_element_type=F32).astype(o_ref.dtype)


def _proj_transposed(xb, wt, batch, seq, tm, tn):
    T, K = xb.shape
    N = wt.shape[0]
    tm, tn = min(tm, seq), min(tn, N)
    ns = seq // tm
    return pl.pallas_call(
        _proj_nt_kernel,
        grid=(N // tn, T // tm),
        in_specs=[pl.BlockSpec((tn, K), lambda j, i: (j, 0)),
                  pl.BlockSpec((tm, K), lambda j, i: (i, 0))],
        out_specs=pl.BlockSpec((None, tn, tm), lambda j, i: (i // ns, j, i % ns)),
        out_shape=jax.ShapeDtypeStruct((batch, N, seq), BF16),
        compiler_params=_cparams(("parallel", "parallel")),
        name="proj_transposed",
    )(wt, xb)


def _proj_conv_kernel(xp_ref, x_ref, xn_ref, w_ref, cw_ref, cb_ref, o_ref, xcat_ref, acc_ref,
                      *, tm, tiles_per_seq):
    i = pl.program_id(1)
    pos = i % tiles_per_seq
    xp = xp_ref[...]
    xn = xn_ref[...]
    xcat_ref[0:HALO, :] = jnp.where(pos == 0, jnp.zeros_like(xp), xp)
    xcat_ref[HALO:HALO + tm, :] = x_ref[...]
    xcat_ref[HALO + tm:, :] = jnp.where(pos == tiles_per_seq - 1, jnp.zeros_like(xn), xn)
    acc_ref[...] = jnp.dot(xcat_ref[...], w_ref[...], preferred_element_type=F32)
    pad = SSD_CONV // 2
    out = cb_ref[...] + cw_ref[0:1, :] * acc_ref[HALO - pad:HALO - pad + tm, :]
    for j in range(1, SSD_CONV):
        out = out + cw_ref[j:j + 1, :] * acc_ref[HALO - pad + j:HALO - pad + j + tm, :]
    o_ref[...] = _silu(out).astype(o_ref.dtype)


def _proj_conv(xb, w, conv_w8, conv_b, seq, out_dtype, tm, tn):
    T, K = xb.shape
    N = w.shape[1]
    tm, tn = min(tm, seq), min(tn, N)
    tps = seq // tm
    hb = tm // HALO
    nhb = T // HALO
    return pl.pallas_call(
        functools.partial(_proj_conv_kernel, tm=tm, tiles_per_seq=tps),
        grid=(N // tn, T // tm),
        in_specs=[pl.BlockSpec((HALO, K), lambda j, i: (jnp.maximum(i * hb - 1, 0), 0)),
                  pl.BlockSpec((tm, K), lambda j, i: (i, 0)),
                  pl.BlockSpec((HALO, K), lambda j, i: (jnp.minimum((i + 1) * hb, nhb - 1), 0)),
                  pl.BlockSpec((K, tn), lambda j, i: (0, j)),
                  pl.BlockSpec((8, tn), lambda j, i: (0, j)),
                  pl.BlockSpec((1, tn), lambda j, i: (0, j))],
        out_specs=pl.BlockSpec((tm, tn), lambda j, i: (i, j)),
        out_shape=jax.ShapeDtypeStruct((T, N), out_dtype),
        scratch_shapes=[pltpu.VMEM((tm + 2 * HALO, K), BF16),
                        pltpu.VMEM((tm + 2 * HALO, tn), F32)],
        compiler_params=_cparams(("parallel", "parallel")),
        name="proj_conv",
    )(xb, xb, xb, w, conv_w8, conv_b)


def _softplus(v):
    return jnp.maximum(v, 0.0) + jnp.log(1.0 + jnp.exp(-jnp.abs(v)))


def _split3(v):
    hi = v.astype(BF16)
    r1 = v - hi.astype(F32)
    mid = r1.astype(BF16)
    lo = (r1 - mid.astype(F32)).astype(BF16)
    return hi, mid, lo


def _ssd_chunk(xs, bm, cm, dt_raw, bias, a_neg, h_ref, y_ref, row0, lane0, forward, dskip):
    Q = SSD_CHUNK
    qi = lax.broadcasted_iota(jnp.int32, (Q, Q), 0)
    si = lax.broadcasted_iota(jnp.int32, (Q, Q), 1)
    if forward:
        keep = si <= qi
        last = Q - 1
    else:
        keep = si >= qi
        last = 0
    tri = jnp.where(keep, 1.0, 0.0).astype(BF16)

    dtc = _softplus(dt_raw + bias)
    a = dtc * a_neg
    hi, mid, lo = _split3(a)
    cum = (jnp.dot(tri, hi, preferred_element_type=F32)
           + jnp.dot(tri, mid, preferred_element_type=F32)
           + jnp.dot(tri, lo, preferred_element_type=F32))
    cum_t = cum.T
    dt_t = dtc.T
    cb = cm.astype(BF16)
    gm = lax.dot_general(cb, bm.astype(BF16), _NT, preferred_element_type=F32)
    bt = bm.T
    lane = lax.broadcasted_iota(jnp.int32, (Q, LANES), 1)
    first_half = lane < SSD_HEAD_DIM

    for pair in range(SSD_HEADS_PER_GROUP // 2):
        cols = slice(pair * LANES, (pair + 1) * LANES)
        xs_pair = xs[:, cols]
        h_pair = h_ref[:, cols]
        y_pair = None
        s_pair = None
        decs = []
        for e in range(2):
            ln = lane0 + 2 * pair + e
            hmask = first_half if e == 0 else jnp.logical_not(first_half)
            colb = jnp.broadcast_to(cum[:, ln:ln + 1], (Q, Q))
            rowb = jnp.broadcast_to(cum_t[ln:ln + 1, :], (Q, Q))
            dtrow = jnp.broadcast_to(dt_t[ln:ln + 1, :], (Q, Q))
            decay = jnp.where(keep, jnp.exp(colb - rowb), 0.0)
            m1 = (gm * decay * dtrow).astype(BF16)
            m2 = (cm * jnp.exp(colb)).astype(BF16)
            lhs = jnp.concatenate([m1, m2], axis=1)
            xr = jnp.where(hmask, xs_pair, 0.0).astype(BF16)
            hr = jnp.where(hmask, h_pair, 0.0).astype(BF16)
            rhs = jnp.concatenate([xr, hr], axis=0)
            yc = jnp.dot(lhs, rhs, preferred_element_type=F32)
            y_pair = yc if y_pair is None else y_pair + yc
            tot = colb[last:last + 1, :]
            wb = (bt * (jnp.exp(tot - rowb) * dtrow)).astype(BF16)
            sc = jnp.dot(wb, xr, preferred_element_type=F32)
            s_pair = sc if s_pair is None else s_pair + sc
            decs.append(jnp.exp(tot))
        dec = jnp.where(first_half[0:1, :], decs[0], decs[1])
        h_ref[:, cols] = h_pair * dec + s_pair
        if dskip is not None:
            y_pair = y_pair + dskip[:, cols] * xs_pair
        y_ref[pl.ds(row0, Q), cols] = y_pair


def _ssd_kernel(xsf_ref, bf_ref, cf_ref, dtf_ref, xsb_ref, bb_ref, cb_ref, dtb_ref,
                bias_ref, alog_ref, dsk_ref, yf_ref, yb_ref, hf_ref, hb_ref, *, nck):
    @pl.when(pl.program_id(2) == 0)
    def _():
        hf_ref[...] = jnp.zeros_like(hf_ref)
        hb_ref[...] = jnp.zeros_like(hb_ref)

    bias = bias_ref[...]
    a_neg = -jnp.exp(alog_ref[...])
    dsk = dsk_ref[...]
    Q = SSD_CHUNK

    def body(j, carry):
        rf = pl.multiple_of(j * Q, Q)
        rb = pl.multiple_of((nck - 1 - j) * Q, Q)
        _ssd_chunk(xsf_ref[pl.ds(rf, Q), :], bf_ref[pl.ds(rf, Q), :], cf_ref[pl.ds(rf, Q), :],
                   dtf_ref[pl.ds(rf, Q), :], bias, a_neg, hf_ref, yf_ref, rf, 0, True, dsk)
        _ssd_chunk(xsb_ref[pl.ds(rb, Q), :], bb_ref[pl.ds(rb, Q), :], cb_ref[pl.ds(rb, Q), :],
                   dtb_ref[pl.ds(rb, Q), :], bias, a_neg, hb_ref, yb_ref, rb,
                   SSD_HEADS_PER_GROUP, False, None)
        return carry

    lax.fori_loop(0, nck, body, 0)


def _ssd(xc, dt, bias, alog, dskip, batch, seq, rb):
    T = xc.shape[0]
    rb = min(rb, seq)
    nb = seq // rb
    nck = rb // SSD_CHUNK
    gw = SSD_HEADS_PER_GROUP * SSD_HEAD_DIM
    b_off = SSD_INNER // LANES
    c_off = b_off + SSD_GROUPS

    def fwd(col):
        return lambda b, g, i: (b * nb + i, col(g))

    def bwd(col):
        return lambda b, g, i: (b * nb + nb - 1 - i, col(g))

    def specs(mk):
        return [pl.BlockSpec((rb, gw), mk(lambda g: g)),
                pl.BlockSpec((rb, LANES), mk(lambda g: b_off + g)),
                pl.BlockSpec((rb, LANES), mk(lambda g: c_off + g)),
                pl.BlockSpec((rb, LANES), mk(lambda g: g))]

    out_sds = jax.ShapeDtypeStruct((T, SSD_INNER), F32)
    return pl.pallas_call(
        functools.partial(_ssd_kernel, nck=nck),
        grid=(batch, SSD_GROUPS, nb),
        in_specs=specs(fwd) + specs(bwd) + [
            pl.BlockSpec((None, 1, LANES), lambda b, g, i: (g, 0, 0)),
            pl.BlockSpec((None, 1, LANES), lambda b, g, i: (g, 0, 0)),
            pl.BlockSpec((1, gw), lambda b, g, i: (0, g))],
        out_specs=[pl.BlockSpec((rb, gw), fwd(lambda g: g)),
                   pl.BlockSpec((rb, gw), bwd(lambda g: g))],
        out_shape=[out_sds, out_sds],
        scratch_shapes=[pltpu.VMEM((SSD_STATE, gw), F32), pltpu.VMEM((SSD_STATE, gw), F32)],
        compiler_params=_cparams(("parallel", "parallel", "arbitrary")),
        name="ssd_scan",
    )(xc, xc, xc, dt, xc, xc, xc, dt, bias, alog, dskip)


def _attn_kernel(lam_ref, q_ref, k_ref, vt_ref, dg_ref, nw_ref, o_ref,
                 qm_ref, m_ref, l_ref, acc_ref, *, tk, nk, out_scale):
    q = q_ref[...]
    lane = lax.broadcasted_iota(jnp.int32, q.shape, 1)
    map0 = (lane % SSD_HEAD_DIM) < (DIFF_HEAD_DIM // 2)
    zero = jnp.zeros_like(q)
    qm_ref[0] = jnp.where(map0, q, zero)
    qm_ref[1] = jnp.where(map0, zero, q)
    m_ref[...] = jnp.full_like(m_ref, -jnp.inf)
    l_ref[...] = jnp.zeros_like(l_ref)
    acc_ref[...] = jnp.zeros_like(acc_ref)

    def body(ki, carry):
        k0 = pl.multiple_of(ki * tk, tk)
        k = k_ref[pl.ds(k0, tk), :]
        vt = vt_ref[:, pl.ds(k0, tk)]
        for c in range(2):
            s = lax.dot_general(k, qm_ref[c], _NT, preferred_element_type=F32)
            m_old = m_ref[c]
            m_new = jnp.maximum(m_old, jnp.max(s, axis=0, keepdims=True))
            alpha = jnp.exp(m_old - m_new)
            p = jnp.exp(s - m_new)
            l_ref[c] = alpha * l_ref[c] + jnp.sum(p, axis=0, keepdims=True)
            acc_ref[c] = acc_ref[c] * alpha + jnp.dot(vt, p.astype(BF16),
                                                      preferred_element_type=F32)
            m_ref[c] = m_new
        return carry

    lax.fori_loop(0, nk, body, 0)

    lam = lam_ref[0, 0]
    o = acc_ref[0] * (1.0 / l_ref[0]) - lam * (acc_ref[1] * (1.0 / l_ref[1]))
    ms = jnp.mean(o * o, axis=0, keepdims=True)
    o = o * lax.rsqrt(ms + NORM_EPS)
    ot = o.T * (nw_ref[...] * out_scale)
    o_ref[...] = (ot * _silu(dg_ref[...])).astype(o_ref.dtype)


def _diff_attention(lam, q, k, vt, dg, norm_w, batch, seq, out_scale, tq, tk):
    T = q.shape[0]
    tq, tk = min(tq, seq), min(tk, seq)
    nq = seq // tq
    nk = seq // tk
    return pl.pallas_call(
        functools.partial(_attn_kernel, tk=tk, nk=nk, out_scale=out_scale),
        grid=(batch, DIFF_HEADS, nq),
        in_specs=[pl.BlockSpec(memory_space=pltpu.SMEM),
                  pl.BlockSpec((tq, LANES), lambda b, h, i: (b * nq + i, h)),
                  pl.BlockSpec((seq, LANES), lambda b, h, i: (b, h)),
                  pl.BlockSpec((None, LANES, seq), lambda b, h, i: (b, h, 0)),
                  pl.BlockSpec((tq, LANES), lambda b, h, i: (b * nq + i, h)),
                  pl.BlockSpec((1, LANES), lambda b, h, i: (0, 0))],
        out_specs=pl.BlockSpec((tq, LANES), lambda b, h, i: (b * nq + i, h)),
        out_shape=jax.ShapeDtypeStruct((T, DIFF_WIDTH), BF16),
        scratch_shapes=[pltpu.VMEM((2, tq, LANES), BF16),
                        pltpu.VMEM((2, 1, tq), F32),
                        pltpu.VMEM((2, 1, tq), F32),
                        pltpu.VMEM((2, LANES, tq), F32)],
        compiler_params=_cparams(("parallel", "parallel", "parallel")),
        name="diff_attention",
    )(lam, q, k, vt, dg, norm_w)


def _tail_kernel(x_ref, xb_ref, yf_ref, yb_ref, yd_ref, kv_ref,
                 wz_ref, wcq_ref, wcg_ref, wgl_ref, wbs_ref, wbd_ref, wbc_ref, wo_ref,
                 nw_ref, gb_ref, lng_ref, lnb_ref, xo_ref, xbo_ref):
    xb = xb_ref[...]

    def gate(idx):
        cols = slice(idx * D_MODEL, (idx + 1) * D_MODEL)
        gl = jnp.dot(xb, wgl_ref[:, cols], preferred_element_type=F32) + gb_ref[:, cols]
        return _sigmoid(gl)

    z = jnp.dot(xb, wz_ref[...], preferred_element_type=F32)
    y = (yf_ref[...] + yb_ref[...]) * _silu(z)
    y = y * lax.rsqrt(jnp.mean(y * y, axis=-1, keepdims=True) + NORM_EPS) * nw_ref[...]
    merged = gate(0) * jnp.dot(y.astype(BF16), wbs_ref[...], preferred_element_type=F32)

    merged = merged + gate(1) * jnp.dot(yd_ref[...], wbd_ref[...], preferred_element_type=F32)

    cq = (jnp.dot(xb, wcq_ref[...], preferred_element_type=F32)
          * (CROSS_HEAD_DIM ** -0.5)).astype(BF16)
    cg = jnp.dot(xb, wcg_ref[...], preferred_element_type=F32)
    outs = []
    for h in range(CROSS_HEADS):
        cols = slice(h * CROSS_HEAD_DIM, (h + 1) * CROSS_HEAD_DIM)
        mk = kv_ref[:, cols]
        mv = kv_ref[:, CROSS_WIDTH + h * CROSS_HEAD_DIM:CROSS_WIDTH + (h + 1) * CROSS_HEAD_DIM]
        s = lax.dot_general(cq[:, cols], mk, _NT, preferred_element_type=F32)
        e = jnp.exp(s - jnp.max(s, axis=-1, keepdims=True))
        p = e * (1.0 / jnp.sum(e, axis=-1, keepdims=True))
        outs.append(jnp.dot(p.astype(BF16), mv, preferred_element_type=F32))
    yc = jnp.concatenate(outs, axis=1) * _silu(cg)
    merged = merged + gate(2) * jnp.dot(yc.astype(BF16), wbc_ref[...],
                                        preferred_element_type=F32)

    out = jnp.dot(merged.astype(BF16), wo_ref[...], preferred_element_type=F32)
    r = DEEPNORM_ALPHA * x_ref[...] + out
    mu = jnp.mean(r, axis=-1, keepdims=True)
    d = r - mu
    var = jnp.mean(d * d, axis=-1, keepdims=True)
    xn = d * lax.rsqrt(var + NORM_EPS) * lng_ref[...] + lnb_ref[...]
    xo_ref[...] = xn
    xbo_ref[...] = xn.astype(BF16)


def _tail(x, xb, yf, yb, yd, kv, wz, wcq, wcg, wgl, wbs, wbd, wbc, wo, nw, gb, lng, lnb,
          seq, tm):
    T = x.shape[0]
    tm = min(tm, seq)
    tps = seq // tm
    mem_tokens = kv.shape[0] // (T // seq)
    row = lambda w: pl.BlockSpec((tm, w), lambda i: (i, 0))
    const = lambda a: pl.BlockSpec(a.shape, lambda i: (0, 0), pipeline_mode=pl.Buffered(1))
    return pl.pallas_call(
        _tail_kernel,
        grid=(T // tm,),
        in_specs=[row(D_MODEL), row(D_MODEL), row(SSD_INNER), row(SSD_INNER), row(DIFF_WIDTH),
                  pl.BlockSpec((mem_tokens, 2 * CROSS_WIDTH), lambda i: (i // tps, 0)),
                  const(wz), const(wcq), const(wcg), const(wgl), const(wbs), const(wbd),
                  const(wbc), const(wo), const(nw), const(gb), const(lng), const(lnb)],
        out_specs=[row(D_MODEL), row(D_MODEL)],
        out_shape=[jax.ShapeDtypeStruct((T, D_MODEL), F32),
                   jax.ShapeDtypeStruct((T, D_MODEL), BF16)],
        compiler_params=_cparams(("parallel",)),
        name="tail",
    )(x, xb, yf, yb, yd, kv, wz, wcq, wcg, wgl, wbs, wbd, wbc, wo, nw, gb, lng, lnb)


def _in_splits():
    sizes = [SSD_INNER, SSD_CONV_CH, 2 * SSD_HEADS, DIFF_WIDTH, DIFF_WIDTH, DIFF_WIDTH,
             DIFF_WIDTH, CROSS_WIDTH, CROSS_WIDTH, N_BRANCH * D_MODEL]
    offs = np.concatenate([[0], np.cumsum(sizes)])
    return [(int(offs[i]), int(offs[i + 1])) for i in range(len(sizes))]


def _rope_perm():
    half = DIFF_HEAD_DIM // 2
    perm = np.zeros(DIFF_WIDTH, np.int32)
    for h in range(DIFF_HEADS):
        for l in range(LANES):
            hf, c, j = l // 64, (l % 64) // half, l % half
            perm[h * LANES + l] = h * LANES + c * DIFF_HEAD_DIM + hf * half + j
    return perm


def _dt_placement():
    cols = np.zeros(2 * SSD_HEADS, np.int32)
    for d in range(2):
        for g in range(SSD_GROUPS):
            for r in range(SSD_HEADS_PER_GROUP):
                cols[d * SSD_HEADS + g * SSD_HEADS_PER_GROUP + r] = (
                    g * LANES + d * SSD_HEADS_PER_GROUP + r)
    return cols


def _per_group_lanes(v):
    out = jnp.zeros((SSD_GROUPS * LANES,), F32).at[_dt_placement()].set(v.reshape(-1).astype(F32))
    return out.reshape(SSD_GROUPS, 1, LANES)


def kernel(x, mem, positions, w_in, conv_w, conv_b, dt_bias, a_log, d_skip, ssd_norm_w,
           diff_lam, diff_norm_w, w_mem_kv, w_br_ssd, w_br_diff, w_br_cross, gate_b, w_out,
           ln_g, ln_b):
    batch, seq, _ = x.shape
    T = batch * seq
    mem_tokens = mem.shape[1]
    (sz, sxbc, sdt, sdq, sdk, sdv, sdg, scq, scg, sgl) = _in_splits()
    perm = _rope_perm()
    dt_cols = _dt_placement()

    inv = 1.0 / (ROPE_THETA ** (jnp.arange(0, DIFF_HEAD_DIM, 2, dtype=F32) / DIFF_HEAD_DIM))
    ang = positions.astype(F32)[..., None] * inv
    cos, sin = jnp.cos(ang), jnp.sin(ang)
    cos_t = jnp.concatenate([cos, cos, cos, cos], axis=-1).reshape(T, LANES)
    sin_t = jnp.concatenate([-sin, -sin, sin, sin], axis=-1).reshape(T, LANES)

    xf = x.reshape(T, D_MODEL)
    xb = xf.astype(BF16)
    memb = mem.reshape(batch * mem_tokens, D_MODEL).astype(BF16)

    for layer in range(DEPTH):
        lambda_init = 0.8 - 0.6 * math.exp(-0.3 * layer)
        w = w_in[layer]
        wb = lambda s: w[:, s[0]:s[1]].astype(BF16)
        w_q = wb(sdq)[:, perm]
        w_k = wb(sdk)[:, perm]
        w_vt = wb(sdv).T
        w_dt = jnp.zeros((D_MODEL, SSD_GROUPS * LANES), BF16).at[:, dt_cols].set(wb(sdt))
        conv_w8 = jnp.zeros((8, SSD_CONV_CH), F32).at[:SSD_CONV].set(conv_w[layer])

        xc = _proj_conv(xb, wb(sxbc), conv_w8, conv_b[layer][None, :], seq, F32, 512, 1024)
        dt = _proj_plain(xb, w_dt, F32, 1024, 512)
        q = _proj_rope(xb, w_q, cos_t, sin_t, DIFF_HEAD_DIM ** -0.5, 1024, 1024)
        k = _proj_rope(xb, w_k, cos_t, sin_t, 1.0, 1024, 1024)
        vt = _proj_transposed(xb, w_vt, batch, seq, 1024, 1024)
        dg = _proj_plain(xb, wb(sdg), F32, 1024, 1024)
        kv = _proj_plain(memb, w_mem_kv[layer].astype(BF16), BF16, 1024, 1024)

        yf, yb = _ssd(xc, dt, _per_group_lanes(dt_bias[layer]), _per_group_lanes(a_log[layer]),
                      jnp.repeat(d_skip[layer].astype(F32), SSD_HEAD_DIM)[None, :],
                      batch, seq, 1024)

        lq = diff_lam[layer].astype(F32)
        lam = (jnp.exp(jnp.sum(lq[0] * lq[1])) - jnp.exp(jnp.sum(lq[2] * lq[3]))
               + lambda_init).reshape(1, 1)
        yd = _diff_attention(lam, q, k, vt, dg, diff_norm_w[layer].astype(F32)[None, :],
                             batch, seq, 1.0 - lambda_init, 512, 512)

        xf, xb = _tail(xf, xb, yf, yb, yd, kv,
                       wb(sz), wb(scq), wb(scg), wb(sgl),
                       w_br_ssd[layer].astype(BF16), w_br_diff[layer].astype(BF16),
                       w_br_cross[layer].astype(BF16), w_out[layer].astype(BF16),
                       ssd_norm_w[layer].astype(F32)[None, :],
                       gate_b[layer].astype(F32).reshape(1, N_BRANCH * D_MODEL),
                       ln_g[layer].astype(F32)[None, :], ln_b[layer].astype(F32)[None, :],
                       seq, 256)

    return xf.reshape(batch, seq, D_MODEL).astype(x.dtype)
```

```python
import functools
import math

import numpy as np
import jax
import jax.numpy as jnp
from jax import lax
from jax.experimental import pallas as pl
from jax.experimental.pallas import tpu as pltpu

F32 = jnp.float32
BF16 = jnp.bfloat16

D_MODEL = 1024
DEPTH = 2
SSD_INNER = 2048
SSD_HEAD_DIM = 64
SSD_HEADS = 32
SSD_GROUPS = 4
SSD_HEADS_PER_GROUP = SSD_HEADS // SSD_GROUPS
SSD_STATE = 128
SSD_CONV = 5
SSD_CHUNK = 128
SSD_CONV_CH = SSD_INNER + 2 * SSD_GROUPS * SSD_STATE
DIFF_HEAD_DIM = 64
DIFF_HEADS = 8
DIFF_WIDTH = 1024
ROPE_THETA = 10000.0
CROSS_HEADS = 4
CROSS_HEAD_DIM = 256
CROSS_WIDTH = 1024
N_BRANCH = 3
DEEPNORM_ALPHA = (2 * DEPTH) ** 0.25
NORM_EPS = 1e-5

LANES = 128
HALO = 16
ONES_ROWS = 16
LOG2E = 1.4426950408889634
VMEM_LIMIT = 56 * 1024 * 1024

_NT = (((1,), (1,)), ((), ()))


def _sigmoid(v):
    return 1.0 / (1.0 + jnp.exp(-v))


def _silu(v):
    return v * _sigmoid(v)


def _cparams(sem):
    return pltpu.CompilerParams(dimension_semantics=sem, vmem_limit_bytes=VMEM_LIMIT)


def _proj_plain_kernel(x_ref, w_ref, o_ref):
    o_ref[...] = jnp.dot(x_ref[...], w_ref[...],
                         preferred_element_type=F32).astype(o_ref.dtype)


def _proj_plain(xb, w, out_dtype, tm, tn):
    T, K = xb.shape
    N = w.shape[1]
    tm, tn = min(tm, T), min(tn, N)
    return pl.pallas_call(
        _proj_plain_kernel,
        grid=(N // tn, T // tm),
        in_specs=[pl.BlockSpec((tm, K), lambda j, i: (i, 0)),
                  pl.BlockSpec((K, tn), lambda j, i: (0, j))],
        out_specs=pl.BlockSpec((tm, tn), lambda j, i: (i, j)),
        out_shape=jax.ShapeDtypeStruct((T, N), out_dtype),
        compiler_params=_cparams(("parallel", "parallel")),
        name="proj_plain",
    )(xb, w)


def _proj_rope_kernel(x_ref, w_ref, cos_ref, sin_ref, o_ref, *, scale):
    acc = jnp.dot(x_ref[...], w_ref[...], preferred_element_type=F32)
    c = cos_ref[...]
    s = sin_ref[...]
    for j in range(acc.shape[1] // LANES):
        t = acc[:, j * LANES:(j + 1) * LANES]
        r = pltpu.roll(t, LANES // 2, axis=1)
        o = t * c + r * s
        if scale != 1.0:
            o = o * scale
        o_ref[:, j * LANES:(j + 1) * LANES] = o.astype(o_ref.dtype)


def _proj_rope(xb, w, cos_t, sin_t, scale, tm, tn):
    T, K = xb.shape
    N = w.shape[1]
    tm, tn = min(tm, T), min(tn, N)
    return pl.pallas_call(
        functools.partial(_proj_rope_kernel, scale=scale),
        grid=(N // tn, T // tm),
        in_specs=[pl.BlockSpec((tm, K), lambda j, i: (i, 0)),
                  pl.BlockSpec((K, tn), lambda j, i: (0, j)),
                  pl.BlockSpec((tm, LANES), lambda j, i: (i, 0)),
                  pl.BlockSpec((tm, LANES), lambda j, i: (i, 0))],
        out_specs=pl.BlockSpec((tm, tn), lambda j, i: (i, j)),
        out_shape=jax.ShapeDtypeStruct((T, N), BF16),
        compiler_params=_cparams(("parallel", "parallel")),
        name="proj_rope",
    )(xb, w, cos_t, sin_t)


def _proj_nt_kernel(wt_ref, x_ref, o_ref):
    o_ref[...] = lax.dot_general(wt_ref[...], x_ref[...], _NT,
                                 preferred_element_type=F32).astype(o_ref.dtype)


def _proj_transposed(xb, wt, batch, seq, tm, tn):
    T, K = xb.shape
    N = wt.shape[0]
    tm, tn = min(tm, seq), min(tn, N)
    ns = seq // tm
    return pl.pallas_call(
        _proj_nt_kernel,
        grid=(N // tn, T // tm),
        in_specs=[pl.BlockSpec((tn, K), lambda j, i: (j, 0)),
                  pl.BlockSpec((tm, K), lambda j, i: (i, 0))],
        out_specs=pl.BlockSpec((None, tn, tm), lambda j, i: (i // ns, j, i % ns)),
        out_shape=jax.ShapeDtypeStruct((batch, N, seq), BF16),
        compiler_params=_cparams(("parallel", "parallel")),
        name="proj_transposed",
    )(wt, xb)


def _proj_conv_kernel(xp_ref, x_ref, xn_ref, w_ref, cw_ref, cb_ref, o_ref, xcat_ref, acc_ref,
                      *, tm, tiles_per_seq):
    i = pl.program_id(1)
    pos = i % tiles_per_seq
    xp = xp_ref[...]
    xn = xn_ref[...]
    xcat_ref[0:HALO, :] = jnp.where(pos == 0, jnp.zeros_like(xp), xp)
    xcat_ref[HALO:HALO + tm, :] = x_ref[...]
    xcat_ref[HALO + tm:, :] = jnp.where(pos == tiles_per_seq - 1, jnp.zeros_like(xn), xn)
    acc_ref[...] = jnp.dot(xcat_ref[...], w_ref[...], preferred_element_type=F32)
    pad = SSD_CONV // 2
    out = cb_ref[...] + cw_ref[0:1, :] * acc_ref[HALO - pad:HALO - pad + tm, :]
    for j in range(1, SSD_CONV):
        out = out + cw_ref[j:j + 1, :] * acc_ref[HALO - pad + j:HALO - pad + j + tm, :]
    o_ref[...] = _silu(out).astype(o_ref.dtype)


def _proj_conv(xb, w, conv_w8, conv_b, seq, out_dtype, tm, tn):
    T, K = xb.shape
    N = w.shape[1]
    tm, tn = min(tm, seq), min(tn, N)
    tps = seq // tm
    hb = tm // HALO
    nhb = T // HALO
    return pl.pallas_call(
        functools.partial(_proj_conv_kernel, tm=tm, tiles_per_seq=tps),
        grid=(N // tn, T // tm),
        in_specs=[pl.BlockSpec((HALO, K), lambda j, i: (jnp.maximum(i * hb - 1, 0), 0)),
                  pl.BlockSpec((tm, K), lambda j, i: (i, 0)),
                  pl.BlockSpec((HALO, K), lambda j, i: (jnp.minimum((i + 1) * hb, nhb - 1), 0)),
                  pl.BlockSpec((K, tn), lambda j, i: (0, j)),
                  pl.BlockSpec((8, tn), lambda j, i: (0, j)),
                  pl.BlockSpec((1, tn), lambda j, i: (0, j))],
        out_specs=pl.BlockSpec((tm, tn), lambda j, i: (i, j)),
        out_shape=jax.ShapeDtypeStruct((T, N), out_dtype),
        scratch_shapes=[pltpu.VMEM((tm + 2 * HALO, K), BF16),
                        pltpu.VMEM((tm + 2 * HALO, tn), F32)],
        compiler_params=_cparams(("parallel", "parallel")),
        name="proj_conv",
    )(xb, xb, xb, w, conv_w8, conv_b)


def _softplus(v):
    return jnp.maximum(v, 0.0) + jnp.log(1.0 + jnp.exp(-jnp.abs(v)))


def _split3(v):
    hi = v.astype(BF16)
    r1 = v - hi.astype(F32)
    mid = r1.astype(BF16)
    lo = (r1 - mid.astype(F32)).astype(BF16)
    return hi, mid, lo


def _ssd_chunk(xs, bm, cm, dt_raw, bias, a_neg, h_ref, y_ref, row0, lane0, forward, dskip):
    Q = SSD_CHUNK
    qi = lax.broadcasted_iota(jnp.int32, (Q, Q), 0)
    si = lax.broadcasted_iota(jnp.int32, (Q, Q), 1)
    if forward:
        keep = si <= qi
        last = Q - 1
    else:
        keep = si >= qi
        last = 0
    tri = jnp.where(keep, 1.0, 0.0).astype(BF16)

    dtc = _softplus(dt_raw + bias)
    a = dtc * a_neg
    hi, mid, lo = _split3(a)
    cum = (jnp.dot(tri, hi, preferred_element_type=F32)
           + jnp.dot(tri, mid, preferred_element_type=F32)
           + jnp.dot(tri, lo, preferred_element_type=F32))
    cum_t = cum.T
    dt_t = dtc.T
    cb = cm.astype(BF16)
    gm = lax.dot_general(cb, bm.astype(BF16), _NT, preferred_element_type=F32)
    bt = bm.T
    lane = lax.broadcasted_iota(jnp.int32, (Q, LANES), 1)
    first_half = lane < SSD_HEAD_DIM

    for pair in range(SSD_HEADS_PER_GROUP // 2):
        cols = slice(pair * LANES, (pair + 1) * LANES)
        xs_pair = xs[:, cols]
        h_pair = h_ref[:, cols]
        y_pair = None
        s_pair = None
        decs = []
        for e in range(2):
            ln = lane0 + 2 * pair + e
            hmask = first_half if e == 0 else jnp.logical_not(first_half)
            colb = jnp.broadcast_to(cum[:, ln:ln + 1], (Q, Q))
            rowb = jnp.broadcast_to(cum_t[ln:ln + 1, :], (Q, Q))
            dtrow = jnp.broadcast_to(dt_t[ln:ln + 1, :], (Q, Q))
            decay = jnp.where(keep, jnp.exp(colb - rowb), 0.0)
            m1 = (gm * decay * dtrow).astype(BF16)
            m2 = (cm * jnp.exp(colb)).astype(BF16)
            lhs = jnp.concatenate([m1, m2], axis=1)
            xr = jnp.where(hmask, xs_pair, 0.0).astype(BF16)
            hr = jnp.where(hmask, h_pair, 0.0).astype(BF16)
            rhs = jnp.concatenate([xr, hr], axis=0)
            yc = jnp.dot(lhs, rhs, preferred_element_type=F32)
            y_pair = yc if y_pair is None else y_pair + yc
            tot = colb[last:last + 1, :]
            wb = (bt * (jnp.exp(tot - rowb) * dtrow)).astype(BF16)
            sc = jnp.dot(wb, xr, preferred_element_type=F32)
            s_pair = sc if s_pair is None else s_pair + sc
            decs.append(jnp.exp(tot))
        dec = jnp.where(first_half[0:1, :], decs[0], decs[1])
        h_ref[:, cols] = h_pair * dec + s_pair
        if dskip is not None:
            y_pair = y_pair + dskip[:, cols] * xs_pair
        y_ref[pl.ds(row0, Q), cols] = y_pair


def _ssd_kernel(xsf_ref, bf_ref, cf_ref, dtf_ref, xsb_ref, bb_ref, cb_ref, dtb_ref,
                bias_ref, alog_ref, dsk_ref, yf_ref, yb_ref, hf_ref, hb_ref, *, nck):
    @pl.when(pl.program_id(2) == 0)
    def _():
        hf_ref[...] = jnp.zeros_like(hf_ref)
        hb_ref[...] = jnp.zeros_like(hb_ref)

    bias = bias_ref[...]
    a_neg = -jnp.exp(alog_ref[...])
    dsk = dsk_ref[...]
    Q = SSD_CHUNK

    def body(j, carry):
        rf = pl.multiple_of(j * Q, Q)
        rb = pl.multiple_of((nck - 1 - j) * Q, Q)
        _ssd_chunk(xsf_ref[pl.ds(rf, Q), :], bf_ref[pl.ds(rf, Q), :], cf_ref[pl.ds(rf, Q), :],
                   dtf_ref[pl.ds(rf, Q), :], bias, a_neg, hf_ref, yf_ref, rf, 0, True, dsk)
        _ssd_chunk(xsb_ref[pl.ds(rb, Q), :], bb_ref[pl.ds(rb, Q), :], cb_ref[pl.ds(rb, Q), :],
                   dtb_ref[pl.ds(rb, Q), :], bias, a_neg, hb_ref, yb_ref, rb,
                   SSD_HEADS_PER_GROUP, False, None)
        return carry

    lax.fori_loop(0, nck, body, 0)


def _ssd(xc, dt, bias, alog, dskip, batch, seq, rb):
    T = xc.shape[0]
    rb = min(rb, seq)
    nb = seq // rb
    nck = rb // SSD_CHUNK
    gw = SSD_HEADS_PER_GROUP * SSD_HEAD_DIM
    b_off = SSD_INNER // LANES
    c_off = b_off + SSD_GROUPS

    def fwd(col):
        return lambda b, g, i: (b * nb + i, col(g))

    def bwd(col):
        return lambda b, g, i: (b * nb + nb - 1 - i, col(g))

    def specs(mk):
        return [pl.BlockSpec((rb, gw), mk(lambda g: g)),
                pl.BlockSpec((rb, LANES), mk(lambda g: b_off + g)),
                pl.BlockSpec((rb, LANES), mk(lambda g: c_off + g)),
                pl.BlockSpec((rb, LANES), mk(lambda g: g))]

    out_sds = jax.ShapeDtypeStruct((T, SSD_INNER), F32)
    return pl.pallas_call(
        functools.partial(_ssd_kernel, nck=nck),
        grid=(batch, SSD_GROUPS, nb),
        in_specs=specs(fwd) + specs(bwd) + [
            pl.BlockSpec((None, 1, LANES), lambda b, g, i: (g, 0, 0)),
            pl.BlockSpec((None, 1, LANES), lambda b, g, i: (g, 0, 0)),
            pl.BlockSpec((1, gw), lambda b, g, i: (0, g))],
        out_specs=[pl.BlockSpec((rb, gw), fwd(lambda g: g)),
                   pl.BlockSpec((rb, gw), bwd(lambda g: g))],
        out_shape=[out_sds, out_sds],
        scratch_shapes=[pltpu.VMEM((SSD_STATE, gw), F32), pltpu.VMEM((SSD_STATE, gw), F32)],
        compiler_params=_cparams(("parallel", "parallel", "arbitrary")),
        name="ssd_scan",
    )(xc, xc, xc, dt, xc, xc, xc, dt, bias, alog, dskip)


def _attn_kernel(lam_ref, q_ref, k_ref, vt_ref, dg_ref, nw_ref, o_ref,
                 qm_ref, s_ref, m_ref, acc_ref, *, tk, nk, out_scale):
    q = q_ref[...]
    lane = lax.broadcasted_iota(jnp.int32, q.shape, 1)
    map0 = (lane % SSD_HEAD_DIM) < (DIFF_HEAD_DIM // 2)
    zero = jnp.zeros_like(q)
    qm_ref[0] = jnp.where(map0, q, zero)
    qm_ref[1] = jnp.where(map0, zero, q)
    m_ref[...] = jnp.full_like(m_ref, -jnp.inf)
    acc_ref[...] = jnp.zeros_like(acc_ref)
    ones = jnp.ones((ONES_ROWS, tk), BF16)
    dv = 2 * DIFF_HEAD_DIM

    def key_start(blk):
        return blk * tk if isinstance(blk, int) else pl.multiple_of(blk * tk, tk)

    def scores(blk, slot):
        k = k_ref[pl.ds(key_start(blk), tk), :]
        for c in range(2):
            s_ref[slot, c] = lax.dot_general(k, qm_ref[c], _NT, preferred_element_type=F32)

    def softmax_pv(blk, slot):
        vt = vt_ref[:, pl.ds(key_start(blk), tk)]
        lhs = jnp.concatenate([vt, ones], axis=0)
        for c in range(2):
            s = s_ref[slot, c]
            m_old = m_ref[c]
            m_new = jnp.maximum(m_old, jnp.max(s, axis=0, keepdims=True))
            alpha = jnp.exp2(m_old - m_new)
            p = jnp.exp2(s - m_new).astype(BF16)
            acc_ref[c] = acc_ref[c] * alpha + jnp.dot(lhs, p, preferred_element_type=F32)
            m_ref[c] = m_new

    scores(0, 0)

    def body(j, carry):
        scores(2 * j + 1, 1)
        softmax_pv(2 * j, 0)
        scores(2 * j + 2, 0)
        softmax_pv(2 * j + 1, 1)
        return carry

    lax.fori_loop(0, nk // 2 - 1, body, 0)
    scores(nk - 1, 1)
    softmax_pv(nk - 2, 0)
    softmax_pv(nk - 1, 1)

    lam = lam_ref[0, 0]
    o = (acc_ref[0, :dv, :] * (1.0 / acc_ref[0, dv:dv + 1, :])
         - lam * (acc_ref[1, :dv, :] * (1.0 / acc_ref[1, dv:dv + 1, :])))
    ms = jnp.mean(o * o, axis=0, keepdims=True)
    o = o * lax.rsqrt(ms + NORM_EPS)
    ot = o.T * (nw_ref[...] * out_scale)
    o_ref[...] = (ot * _silu(dg_ref[...])).astype(o_ref.dtype)


def _diff_attention(lam, q, k, vt, dg, norm_w, batch, seq, out_scale, tq, tk):
    T = q.shape[0]
    tq, tk = min(tq, seq), min(tk, seq // 2)
    nq = seq // tq
    nk = seq // tk
    assert nk % 2 == 0 and nk * tk == seq
    return pl.pallas_call(
        functools.partial(_attn_kernel, tk=tk, nk=nk, out_scale=out_scale),
        grid=(batch, DIFF_HEADS, nq),
        in_specs=[pl.BlockSpec(memory_space=pltpu.SMEM),
                  pl.BlockSpec((tq, LANES), lambda b, h, i: (b * nq + i, h)),
                  pl.BlockSpec((seq, LANES), lambda b, h, i: (b, h)),
                  pl.BlockSpec((None, LANES, seq), lambda b, h, i: (b, h, 0)),
                  pl.BlockSpec((tq, LANES), lambda b, h, i: (b * nq + i, h)),
                  pl.BlockSpec((1, LANES), lambda b, h, i: (0, 0))],
        out_specs=pl.BlockSpec((tq, LANES), lambda b, h, i: (b * nq + i, h)),
        out_shape=jax.ShapeDtypeStruct((T, DIFF_WIDTH), BF16),
        scratch_shapes=[pltpu.VMEM((2, tq, LANES), BF16),
                        pltpu.VMEM((2, 2, tk, tq), F32),
                        pltpu.VMEM((2, 1, tq), F32),
                        pltpu.VMEM((2, LANES + ONES_ROWS, tq), F32)],
        compiler_params=_cparams(("parallel", "parallel", "parallel")),
        name="diff_attention",
    )(lam, q, k, vt, dg, norm_w)


def _tail_kernel(x_ref, xb_ref, yf_ref, yb_ref, yd_ref, kv_ref,
                 wz_ref, wcq_ref, wcg_ref, wgl_ref, wbs_ref, wbd_ref, wbc_ref, wo_ref,
                 nw_ref, gb_ref, lng_ref, lnb_ref, xo_ref, xbo_ref):
    xb = xb_ref[...]

    def gate(idx):
        cols = slice(idx * D_MODEL, (idx + 1) * D_MODEL)
        gl = jnp.dot(xb, wgl_ref[:, cols], preferred_element_type=F32) + gb_ref[:, cols]
        return _sigmoid(gl)

    z = jnp.dot(xb, wz_ref[...], preferred_element_type=F32)
    y = (yf_ref[...] + yb_ref[...]) * _silu(z)
    y = y * lax.rsqrt(jnp.mean(y * y, axis=-1, keepdims=True) + NORM_EPS) * nw_ref[...]
    merged = gate(0) * jnp.dot(y.astype(BF16), wbs_ref[...], preferred_element_type=F32)

    merged = merged + gate(1) * jnp.dot(yd_ref[...], wbd_ref[...], preferred_element_type=F32)

    cq = (jnp.dot(xb, wcq_ref[...], preferred_element_type=F32)
          * (CROSS_HEAD_DIM ** -0.5)).astype(BF16)
    cg = jnp.dot(xb, wcg_ref[...], preferred_element_type=F32)
    outs = []
    for h in range(CROSS_HEADS):
        cols = slice(h * CROSS_HEAD_DIM, (h + 1) * CROSS_HEAD_DIM)
        mk = kv_ref[:, cols]
        mv = kv_ref[:, CROSS_WIDTH + h * CROSS_HEAD_DIM:CROSS_WIDTH + (h + 1) * CROSS_HEAD_DIM]
        s = lax.dot_general(cq[:, cols], mk, _NT, preferred_element_type=F32)
        e = jnp.exp(s - jnp.max(s, axis=-1, keepdims=True))
        p = e * (1.0 / jnp.sum(e, axis=-1, keepdims=True))
        outs.append(jnp.dot(p.astype(BF16), mv, preferred_element_type=F32))
    yc = jnp.concatenate(outs, axis=1) * _silu(cg)
    merged = merged + gate(2) * jnp.dot(yc.astype(BF16), wbc_ref[...],
                                        preferred_element_type=F32)

    out = jnp.dot(merged.astype(BF16), wo_ref[...], preferred_element_type=F32)
    r = DEEPNORM_ALPHA * x_ref[...] + out
    mu = jnp.mean(r, axis=-1, keepdims=True)
    d = r - mu
    var = jnp.mean(d * d, axis=-1, keepdims=True)
    xn = d * lax.rsqrt(var + NORM_EPS) * lng_ref[...] + lnb_ref[...]
    xo_ref[...] = xn
    xbo_ref[...] = xn.astype(BF16)


def _tail(x, xb, yf, yb, yd, kv, wz, wcq, wcg, wgl, wbs, wbd, wbc, wo, nw, gb, lng, lnb,
          seq, tm):
    T = x.shape[0]
    tm = min(tm, seq)
    tps = seq // tm
    mem_tokens = kv.shape[0] // (T // seq)
    row = lambda w: pl.BlockSpec((tm, w), lambda i: (i, 0))
    const = lambda a: pl.BlockSpec(a.shape, lambda i: (0, 0), pipeline_mode=pl.Buffered(1))
    return pl.pallas_call(
        _tail_kernel,
        grid=(T // tm,),
        in_specs=[row(D_MODEL), row(D_MODEL), row(SSD_INNER), row(SSD_INNER), row(DIFF_WIDTH),
                  pl.BlockSpec((mem_tokens, 2 * CROSS_WIDTH), lambda i: (i // tps, 0)),
                  const(wz), const(wcq), const(wcg), const(wgl), const(wbs), const(wbd),
                  const(wbc), const(wo), const(nw), const(gb), const(lng), const(lnb)],
        out_specs=[row(D_MODEL), row(D_MODEL)],
        out_shape=[jax.ShapeDtypeStruct((T, D_MODEL), F32),
                   jax.ShapeDtypeStruct((T, D_MODEL), BF16)],
        compiler_params=_cparams(("parallel",)),
        name="tail",
    )(x, xb, yf, yb, yd, kv, wz, wcq, wcg, wgl, wbs, wbd, wbc, wo, nw, gb, lng, lnb)


def _in_splits():
    sizes = [SSD_INNER, SSD_CONV_CH, 2 * SSD_HEADS, DIFF_WIDTH, DIFF_WIDTH, DIFF_WIDTH,
             DIFF_WIDTH, CROSS_WIDTH, CROSS_WIDTH, N_BRANCH * D_MODEL]
    offs = np.concatenate([[0], np.cumsum(sizes)])
    return [(int(offs[i]), int(offs[i + 1])) for i in range(len(sizes))]


def _rope_perm():
    half = DIFF_HEAD_DIM // 2
    perm = np.zeros(DIFF_WIDTH, np.int32)
    for h in range(DIFF_HEADS):
        for l in range(LANES):
            hf, c, j = l // 64, (l % 64) // half, l % half
            perm[h * LANES + l] = h * LANES + c * DIFF_HEAD_DIM + hf * half + j
    return perm


def _dt_placement():
    cols = np.zeros(2 * SSD_HEADS, np.int32)
    for d in range(2):
        for g in range(SSD_GROUPS):
            for r in range(SSD_HEADS_PER_GROUP):
                cols[d * SSD_HEADS + g * SSD_HEADS_PER_GROUP + r] = (
                    g * LANES + d * SSD_HEADS_PER_GROUP + r)
    return cols


def _per_group_lanes(v):
    out = jnp.zeros((SSD_GROUPS * LANES,), F32).at[_dt_placement()].set(v.reshape(-1).astype(F32))
    return out.reshape(SSD_GROUPS, 1, LANES)


def kernel(x, mem, positions, w_in, conv_w, conv_b, dt_bias, a_log, d_skip, ssd_norm_w,
           diff_lam, diff_norm_w, w_mem_kv, w_br_ssd, w_br_diff, w_br_cross, gate_b, w_out,
           ln_g, ln_b):
    batch, seq, _ = x.shape
    T = batch * seq
    mem_tokens = mem.shape[1]
    (sz, sxbc, sdt, sdq, sdk, sdv, sdg, scq, scg, sgl) = _in_splits()
    perm = _rope_perm()
    dt_cols = _dt_placement()

    inv = 1.0 / (ROPE_THETA ** (jnp.arange(0, DIFF_HEAD_DIM, 2, dtype=F32) / DIFF_HEAD_DIM))
    ang = positions.astype(F32)[..., None] * inv
    cos, sin = jnp.cos(ang), jnp.sin(ang)
    cos_t = jnp.concatenate([cos, cos, cos, cos], axis=-1).reshape(T, LANES)
    sin_t = jnp.concatenate([-sin, -sin, sin, sin], axis=-1).reshape(T, LANES)

    xf = x.reshape(T, D_MODEL)
    xb = xf.astype(BF16)
    memb = mem.reshape(batch * mem_tokens, D_MODEL).astype(BF16)

    for layer in range(DEPTH):
        lambda_init = 0.8 - 0.6 * math.exp(-0.3 * layer)
        w = w_in[layer]
        wb = lambda s: w[:, s[0]:s[1]].astype(BF16)
        w_q = wb(sdq)[:, perm]
        w_k = wb(sdk)[:, perm]
        w_vt = wb(sdv).T
        w_dt = jnp.zeros((D_MODEL, SSD_GROUPS * LANES), BF16).at[:, dt_cols].set(wb(sdt))
        conv_w8 = jnp.zeros((8, SSD_CONV_CH), F32).at[:SSD_CONV].set(conv_w[layer])

        xc = _proj_conv(xb, wb(sxbc), conv_w8, conv_b[layer][None, :], seq, F32, 512, 1024)
        dt = _proj_plain(xb, w_dt, F32, 1024, 512)
        q = _proj_rope(xb, w_q, cos_t, sin_t, DIFF_HEAD_DIM ** -0.5 * LOG2E, 1024, 1024)
        k = _proj_rope(xb, w_k, cos_t, sin_t, 1.0, 1024, 1024)
        vt = _proj_transposed(xb, w_vt, batch, seq, 1024, 1024)
        dg = _proj_plain(xb, wb(sdg), F32, 1024, 1024)
        kv = _proj_plain(memb, w_mem_kv[layer].astype(BF16), BF16, 1024, 1024)

        yf, yb = _ssd(xc, dt, _per_group_lanes(dt_bias[layer]), _per_group_lanes(a_log[layer]),
                      jnp.repeat(d_skip[layer].astype(F32), SSD_HEAD_DIM)[None, :],
                      batch, seq, 1024)

        lq = diff_lam[layer].astype(F32)
        lam = (jnp.exp(jnp.sum(lq[0] * lq[1])) - jnp.exp(jnp.sum(lq[2] * lq[3]))
               + lambda_init).reshape(1, 1)
        yd = _diff_attention(lam, q, k, vt, dg, diff_norm_w[layer].astype(F32)[None, :],
                             batch, seq, 1.0 - lambda_init, 512, 512)

        xf, xb = _tail(xf, xb, yf, yb, yd, kv,
                       wb(sz), wb(scq), wb(scg), wb(sgl),
                       w_br_ssd[layer].astype(BF16), w_br_diff[layer].astype(BF16),
                       w_br_cross[layer].astype(BF16), w_out[layer].astype(BF16),
                       ssd_norm_w[layer].astype(F32)[None, :],
                       gate_b[layer].astype(F32).reshape(1, N_BRANCH * D_MODEL),
                       ln_g[layer].astype(F32)[None, :], ln_b[layer].astype(F32)[None, :],
                       seq, 256)

    return xf.reshape(batch, seq, D_MODEL).astype(x.dtype)
```

```python
import functools
import math

import numpy as np
import jax
import jax.numpy as jnp
from jax import lax
from jax.experimental import pallas as pl
from jax.experimental.pallas import tpu as pltpu

F32 = jnp.float32
BF16 = jnp.bfloat16

D_MODEL = 1024
DEPTH = 2
SSD_INNER = 2048
SSD_HEAD_DIM = 64
SSD_HEADS = 32
SSD_GROUPS = 4
SSD_HEADS_PER_GROUP = SSD_HEADS // SSD_GROUPS
SSD_STATE = 128
SSD_CONV = 5
SSD_CHUNK = 128
SSD_CONV_CH = SSD_INNER + 2 * SSD_GROUPS * SSD_STATE
DIFF_HEAD_DIM = 64
DIFF_HEADS = 8
DIFF_WIDTH = 1024
ROPE_THETA = 10000.0
CROSS_HEADS = 4
CROSS_HEAD_DIM = 256
CROSS_WIDTH = 1024
N_BRANCH = 3
DEEPNORM_ALPHA = (2 * DEPTH) ** 0.25
NORM_EPS = 1e-5

LANES = 128
HALO = 16
ONES_ROWS = 16
LOG2E = 1.4426950408889634
VMEM_LIMIT = 56 * 1024 * 1024

_NT = (((1,), (1,)), ((), ()))


def _sigmoid(v):
    return 1.0 / (1.0 + jnp.exp(-v))


def _silu(v):
    return v * _sigmoid(v)


def _cparams(sem):
    return pltpu.CompilerParams(dimension_semantics=sem, vmem_limit_bytes=VMEM_LIMIT)


def _proj_plain_kernel(x_ref, w_ref, o_ref):
    o_ref[...] = jnp.dot(x_ref[...], w_ref[...],
                         preferred_element_type=F32).astype(o_ref.dtype)


def _proj_plain(xb, w, out_dtype, tm, tn):
    T, K = xb.shape
    N = w.shape[1]
    tm, tn = min(tm, T), min(tn, N)
    return pl.pallas_call(
        _proj_plain_kernel,
        grid=(N // tn, T // tm),
        in_specs=[pl.BlockSpec((tm, K), lambda j, i: (i, 0)),
                  pl.BlockSpec((K, tn), lambda j, i: (0, j))],
        out_specs=pl.BlockSpec((tm, tn), lambda j, i: (i, j)),
        out_shape=jax.ShapeDtypeStruct((T, N), out_dtype),
        compiler_params=_cparams(("parallel", "parallel")),
        name="proj_plain",
    )(xb, w)


def _proj_rope_kernel(x_ref, w_ref, cos_ref, sin_ref, o_ref, *, scale):
    acc = jnp.dot(x_ref[...], w_ref[...], preferred_element_type=F32)
    c = cos_ref[...]
    s = sin_ref[...]
    for j in range(acc.shape[1] // LANES):
        t = acc[:, j * LANES:(j + 1) * LANES]
        r = pltpu.roll(t, LANES // 2, axis=1)
        o = t * c + r * s
        if scale != 1.0:
            o = o * scale
        o_ref[:, j * LANES:(j + 1) * LANES] = o.astype(o_ref.dtype)


def _proj_rope(xb, w, cos_t, sin_t, scale, tm, tn):
    T, K = xb.shape
    N = w.shape[1]
    tm, tn = min(tm, T), min(tn, N)
    return pl.pallas_call(
        functools.partial(_proj_rope_kernel, scale=scale),
        grid=(N // tn, T // tm),
        in_specs=[pl.BlockSpec((tm, K), lambda j, i: (i, 0)),
                  pl.BlockSpec((K, tn), lambda j, i: (0, j)),
                  pl.BlockSpec((tm, LANES), lambda j, i: (i, 0)),
                  pl.BlockSpec((tm, LANES), lambda j, i: (i, 0))],
        out_specs=pl.BlockSpec((tm, tn), lambda j, i: (i, j)),
        out_shape=jax.ShapeDtypeStruct((T, N), BF16),
        compiler_params=_cparams(("parallel", "parallel")),
        name="proj_rope",
    )(xb, w, cos_t, sin_t)


def _proj_nt_kernel(wt_ref, x_ref, o_ref):
    o_ref[...] = lax.dot_general(wt_ref[...], x_ref[...], _NT,
                                 preferred_element_type=F32).astype(o_ref.dtype)


def _proj_transposed(xb, wt, batch, seq, tm, tn):
    T, K = xb.shape
    N = wt.shape[0]
    tm, tn = min(tm, seq), min(tn, N)
    ns = seq // tm
    return pl.pallas_call(
        _proj_nt_kernel,
        grid=(N // tn, T // tm),
        in_specs=[pl.BlockSpec((tn, K), lambda j, i: (j, 0)),
                  pl.BlockSpec((tm, K), lambda j, i: (i, 0))],
        out_specs=pl.BlockSpec((None, tn, tm), lambda j, i: (i // ns, j, i % ns)),
        out_shape=jax.ShapeDtypeStruct((batch, N, seq), BF16),
        compiler_params=_cparams(("parallel", "parallel")),
        name="proj_transposed",
    )(wt, xb)


def _proj_conv_kernel(xp_ref, x_ref, xn_ref, w_ref, cw_ref, cb_ref, o_ref, xcat_ref,
                      *, tm, tiles_per_seq):
    i = pl.program_id(1)
    pos = i % tiles_per_seq
    xp = xp_ref[...]
    xn = xn_ref[...]
    xcat_ref[0:HALO, :] = jnp.where(pos == 0, jnp.zeros_like(xp), xp)
    xcat_ref[HALO:HALO + tm, :] = x_ref[...]
    xcat_ref[HALO + tm:, :] = jnp.where(pos == tiles_per_seq - 1, jnp.zeros_like(xn), xn)
    acc = jnp.dot(xcat_ref[...], w_ref[...], preferred_element_type=F32)
    rows = acc.shape[0]
    pad = SSD_CONV // 2
    out = cb_ref[...]
    for j in range(SSD_CONV):
        shifted = acc if j == pad else pltpu.roll(acc, (pad - j) % rows, axis=0)
        out = out + cw_ref[j:j + 1, :] * shifted[HALO:HALO + tm, :]
    o_ref[...] = _silu(out).astype(o_ref.dtype)


def _proj_conv(xb, w, conv_w8, conv_b, seq, out_dtype, tm, tn):
    T, K = xb.shape
    N = w.shape[1]
    tm, tn = min(tm, seq), min(tn, N)
    tps = seq // tm
    hb = tm // HALO
    nhb = T // HALO
    return pl.pallas_call(
        functools.partial(_proj_conv_kernel, tm=tm, tiles_per_seq=tps),
        grid=(N // tn, T // tm),
        in_specs=[pl.BlockSpec((HALO, K), lambda j, i: (jnp.maximum(i * hb - 1, 0), 0)),
                  pl.BlockSpec((tm, K), lambda j, i: (i, 0)),
                  pl.BlockSpec((HALO, K), lambda j, i: (jnp.minimum((i + 1) * hb, nhb - 1), 0)),
                  pl.BlockSpec((K, tn), lambda j, i: (0, j)),
                  pl.BlockSpec((8, tn), lambda j, i: (0, j)),
                  pl.BlockSpec((1, tn), lambda j, i: (0, j))],
        out_specs=pl.BlockSpec((tm, tn), lambda j, i: (i, j)),
        out_shape=jax.ShapeDtypeStruct((T, N), out_dtype),
        scratch_shapes=[pltpu.VMEM((tm + 2 * HALO, K), BF16)],
        compiler_params=_cparams(("parallel", "parallel")),
        name="proj_conv",
    )(xb, xb, xb, w, conv_w8, conv_b)


def _softplus(v):
    return jnp.maximum(v, 0.0) + jnp.log(1.0 + jnp.exp(-jnp.abs(v)))


def _split3(v):
    hi = v.astype(BF16)
    r1 = v - hi.astype(F32)
    mid = r1.astype(BF16)
    lo = (r1 - mid.astype(F32)).astype(BF16)
    return hi, mid, lo


def _ssd_keep(forward):
    qi = lax.broadcasted_iota(jnp.int32, (SSD_CHUNK, SSD_CHUNK), 0)
    si = lax.broadcasted_iota(jnp.int32, (SSD_CHUNK, SSD_CHUNK), 1)
    return si <= qi if forward else si >= qi


def _ssd_prepare(bm, cm, dt_raw, bias, a_neg, forward, pre_ref):
    tri = jnp.where(_ssd_keep(forward), 1.0, 0.0).astype(BF16)
    dtc = _softplus(dt_raw + bias)
    a = dtc * a_neg
    hi, mid, lo = _split3(a)
    cum = (jnp.dot(tri, hi, preferred_element_type=F32)
           + jnp.dot(tri, mid, preferred_element_type=F32)
           + jnp.dot(tri, lo, preferred_element_type=F32))
    pre_ref[0] = cum
    pre_ref[1] = cum.T - jnp.log(dtc.T)
    pre_ref[2] = lax.dot_general(cm.astype(BF16), bm.astype(BF16), _NT,
                                 preferred_element_type=F32)
    pre_ref[3] = bm.T


def _ssd_main(xs, cm, pre_ref, h_ref, y_ref, row0, lane0, forward, dskip):
    Q = SSD_CHUNK
    keep = _ssd_keep(forward)
    last = Q - 1 if forward else 0
    cum = pre_ref[0]
    row_t = pre_ref[1]
    gm = pre_ref[2]
    bt = pre_ref[3]
    lane = lax.broadcasted_iota(jnp.int32, (Q, LANES), 1)
    first_half = lane < SSD_HEAD_DIM

    for pair in range(SSD_HEADS_PER_GROUP // 2):
        cols = slice(pair * LANES, (pair + 1) * LANES)
        xs_pair = xs[:, cols]
        h_pair = h_ref[:, cols]
        y_pair = None
        s_pair = None
        decs = []
        for e in range(2):
            ln = lane0 + 2 * pair + e
            hmask = first_half if e == 0 else jnp.logical_not(first_half)
            colb = jnp.broadcast_to(cum[:, ln:ln + 1], (Q, Q))
            rowb = jnp.broadcast_to(row_t[ln:ln + 1, :], (Q, Q))
            decay = jnp.where(keep, jnp.exp(colb - rowb), 0.0)
            m1 = (gm * decay).astype(BF16)
            m2 = (cm * jnp.exp(colb)).astype(BF16)
            lhs = jnp.concatenate([m1, m2], axis=1)
            xr = jnp.where(hmask, xs_pair, 0.0).astype(BF16)
            hr = jnp.where(hmask, h_pair, 0.0).astype(BF16)
            rhs = jnp.concatenate([xr, hr], axis=0)
            yc = jnp.dot(lhs, rhs, preferred_element_type=F32)
            y_pair = yc if y_pair is None else y_pair + yc
            tot = colb[last:last + 1, :]
            wb = (bt * jnp.exp(tot - rowb)).astype(BF16)
            sc = jnp.dot(wb, xr, preferred_element_type=F32)
            s_pair = sc if s_pair is None else s_pair + sc
            decs.append(jnp.exp(tot))
        dec = jnp.where(first_half[0:1, :], decs[0], decs[1])
        h_ref[:, cols] = h_pair * dec + s_pair
        if dskip is not None:
            y_pair = y_pair + dskip[:, cols] * xs_pair
        y_ref[pl.ds(row0, Q), cols] = y_pair


def _ssd_kernel(xsf_ref, bf_ref, cf_ref, dtf_ref, xsb_ref, bb_ref, cb_ref, dtb_ref,
                bias_ref, alog_ref, dsk_ref, yf_ref, yb_ref, hf_ref, hb_ref, pre_ref, *, nck):
    @pl.when(pl.program_id(2) == 0)
    def _():
        hf_ref[...] = jnp.zeros_like(hf_ref)
        hb_ref[...] = jnp.zeros_like(hb_ref)

    bias = bias_ref[...]
    a_neg = -jnp.exp(alog_ref[...])
    dsk = dsk_ref[...]
    Q = SSD_CHUNK

    def rows(j):
        if isinstance(j, int):
            return j * Q, (nck - 1 - j) * Q
        return pl.multiple_of(j * Q, Q), pl.multiple_of((nck - 1 - j) * Q, Q)

    def prepare(j, slot):
        rf, rb = rows(j)
        _ssd_prepare(bf_ref[pl.ds(rf, Q), :], cf_ref[pl.ds(rf, Q), :], dtf_ref[pl.ds(rf, Q), :],
                     bias, a_neg, True, pre_ref.at[slot, 0])
        _ssd_prepare(bb_ref[pl.ds(rb, Q), :], cb_ref[pl.ds(rb, Q), :], dtb_ref[pl.ds(rb, Q), :],
                     bias, a_neg, False, pre_ref.at[slot, 1])

    def main(j, slot):
        rf, rb = rows(j)
        _ssd_main(xsf_ref[pl.ds(rf, Q), :], cf_ref[pl.ds(rf, Q), :], pre_ref.at[slot, 0],
                  hf_ref, yf_ref, rf, 0, True, dsk)
        _ssd_main(xsb_ref[pl.ds(rb, Q), :], cb_ref[pl.ds(rb, Q), :], pre_ref.at[slot, 1],
                  hb_ref, yb_ref, rb, SSD_HEADS_PER_GROUP, False, None)

    prepare(0, 0)

    def body(t, carry):
        prepare(2 * t + 1, 1)
        main(2 * t, 0)
        prepare(jnp.minimum(2 * t + 2, nck - 1), 0)
        main(2 * t + 1, 1)
        return carry

    lax.fori_loop(0, nck // 2, body, 0)


def _ssd(xc, dt, bias, alog, dskip, batch, seq, rb):
    T = xc.shape[0]
    rb = min(rb, seq)
    nb = seq // rb
    nck = rb // SSD_CHUNK
    assert nck % 2 == 0 and nb * rb == seq
    gw = SSD_HEADS_PER_GROUP * SSD_HEAD_DIM
    b_off = SSD_INNER // LANES
    c_off = b_off + SSD_GROUPS

    def fwd(col):
        return lambda b, g, i: (b * nb + i, col(g))

    def bwd(col):
        return lambda b, g, i: (b * nb + nb - 1 - i, col(g))

    def specs(mk):
        return [pl.BlockSpec((rb, gw), mk(lambda g: g)),
                pl.BlockSpec((rb, LANES), mk(lambda g: b_off + g)),
                pl.BlockSpec((rb, LANES), mk(lambda g: c_off + g)),
                pl.BlockSpec((rb, LANES), mk(lambda g: g))]

    out_sds = jax.ShapeDtypeStruct((T, SSD_INNER), F32)
    return pl.pallas_call(
        functools.partial(_ssd_kernel, nck=nck),
        grid=(batch, SSD_GROUPS, nb),
        in_specs=specs(fwd) + specs(bwd) + [
            pl.BlockSpec((None, 1, LANES), lambda b, g, i: (g, 0, 0)),
            pl.BlockSpec((None, 1, LANES), lambda b, g, i: (g, 0, 0)),
            pl.BlockSpec((1, gw), lambda b, g, i: (0, g))],
        out_specs=[pl.BlockSpec((rb, gw), fwd(lambda g: g)),
                   pl.BlockSpec((rb, gw), bwd(lambda g: g))],
        out_shape=[out_sds, out_sds],
        scratch_shapes=[pltpu.VMEM((SSD_STATE, gw), F32), pltpu.VMEM((SSD_STATE, gw), F32),
                        pltpu.VMEM((2, 2, 4, SSD_CHUNK, LANES), F32)],
        compiler_params=_cparams(("parallel", "parallel", "arbitrary")),
        name="ssd_scan",
    )(xc, xc, xc, dt, xc, xc, xc, dt, bias, alog, dskip)


def _attn_kernel(lam_ref, q_ref, k_ref, vt_ref, dg_ref, nw_ref, o_ref,
                 qm_ref, s_ref, bm_ref, m_ref, acc_ref, *, tk, nk, out_scale):
    q = q_ref[...]
    lane = lax.broadcasted_iota(jnp.int32, q.shape, 1)
    map0 = (lane % SSD_HEAD_DIM) < (DIFF_HEAD_DIM // 2)
    zero = jnp.zeros_like(q)
    qm_ref[0] = jnp.where(map0, q, zero)
    qm_ref[1] = jnp.where(map0, zero, q)
    m_ref[...] = jnp.full_like(m_ref, -jnp.inf)
    acc_ref[...] = jnp.zeros_like(acc_ref)
    ones = jnp.ones((ONES_ROWS, tk), BF16)
    dv = 2 * DIFF_HEAD_DIM

    def key_start(blk):
        return blk * tk if isinstance(blk, int) else pl.multiple_of(blk * tk, tk)

    def scores(blk, slot):
        k = k_ref[pl.ds(key_start(blk), tk), :]
        for c in range(2):
            s = lax.dot_general(k, qm_ref[c], _NT, preferred_element_type=F32)
            s_ref[slot, c] = s
            bm_ref[slot, c] = jnp.max(s, axis=0, keepdims=True)

    def softmax_pv(blk, slot):
        vt = vt_ref[:, pl.ds(key_start(blk), tk)]
        lhs = jnp.concatenate([vt, ones], axis=0)
        for c in range(2):
            s = s_ref[slot, c]
            m_old = m_ref[c]
            m_new = jnp.maximum(m_old, bm_ref[slot, c])
            alpha = jnp.exp2(m_old - m_new)
            p = jnp.exp2(s - m_new).astype(BF16)
            acc_ref[c] = acc_ref[c] * alpha + jnp.dot(lhs, p, preferred_element_type=F32)
            m_ref[c] = m_new

    scores(0, 0)

    def body(j, carry):
        scores(2 * j + 1, 1)
        softmax_pv(2 * j, 0)
        scores(2 * j + 2, 0)
        softmax_pv(2 * j + 1, 1)
        return carry

    lax.fori_loop(0, nk // 2 - 1, body, 0)
    scores(nk - 1, 1)
    softmax_pv(nk - 2, 0)
    softmax_pv(nk - 1, 1)

    lam = lam_ref[0, 0]
    o = (acc_ref[0, :dv, :] * (1.0 / acc_ref[0, dv:dv + 1, :])
         - lam * (acc_ref[1, :dv, :] * (1.0 / acc_ref[1, dv:dv + 1, :])))
    ms = jnp.mean(o * o, axis=0, keepdims=True)
    o = o * lax.rsqrt(ms + NORM_EPS)
    ot = o.T * (nw_ref[...] * out_scale)
    o_ref[...] = (ot * _silu(dg_ref[...])).astype(o_ref.dtype)


def _diff_attention(lam, q, k, vt, dg, norm_w, batch, seq, out_scale, tq, tk):
    T = q.shape[0]
    tq, tk = min(tq, seq), min(tk, seq // 2)
    nq = seq // tq
    nk = seq // tk
    assert nk % 2 == 0 and nk * tk == seq
    return pl.pallas_call(
        functools.partial(_attn_kernel, tk=tk, nk=nk, out_scale=out_scale),
        grid=(batch, DIFF_HEADS, nq),
        in_specs=[pl.BlockSpec(memory_space=pltpu.SMEM),
                  pl.BlockSpec((tq, LANES), lambda b, h, i: (b * nq + i, h)),
                  pl.BlockSpec((seq, LANES), lambda b, h, i: (b, h)),
                  pl.BlockSpec((None, LANES, seq), lambda b, h, i: (b, h, 0)),
                  pl.BlockSpec((tq, LANES), lambda b, h, i: (b * nq + i, h)),
                  pl.BlockSpec((1, LANES), lambda b, h, i: (0, 0))],
        out_specs=pl.BlockSpec((tq, LANES), lambda b, h, i: (b * nq + i, h)),
        out_shape=jax.ShapeDtypeStruct((T, DIFF_WIDTH), BF16),
        scratch_shapes=[pltpu.VMEM((2, tq, LANES), BF16),
                        pltpu.VMEM((2, 2, tk, tq), F32),
                        pltpu.VMEM((2, 2, 1, tq), F32),
                        pltpu.VMEM((2, 1, tq), F32),
                        pltpu.VMEM((2, LANES + ONES_ROWS, tq), F32)],
        compiler_params=_cparams(("parallel", "parallel", "parallel")),
        name="diff_attention",
    )(lam, q, k, vt, dg, norm_w)


def _tail_kernel(x_ref, xb_ref, yf_ref, yb_ref, yd_ref, kv_ref,
                 wz_ref, wcq_ref, wcg_ref, wgl_ref, wbs_ref, wbd_ref, wbc_ref, wo_ref,
                 nw_ref, gb_ref, lng_ref, lnb_ref, xo_ref, xbo_ref):
    xb = xb_ref[...]

    def gate(idx):
        cols = slice(idx * D_MODEL, (idx + 1) * D_MODEL)
        gl = jnp.dot(xb, wgl_ref[:, cols], preferred_element_type=F32) + gb_ref[:, cols]
        return _sigmoid(gl)

    z = jnp.dot(xb, wz_ref[...], preferred_element_type=F32)
    y = (yf_ref[...] + yb_ref[...]) * _silu(z)
    y = y * lax.rsqrt(jnp.mean(y * y, axis=-1, keepdims=True) + NORM_EPS) * nw_ref[...]
    merged = gate(0) * jnp.dot(y.astype(BF16), wbs_ref[...], preferred_element_type=F32)

    merged = merged + gate(1) * jnp.dot(yd_ref[...], wbd_ref[...], preferred_element_type=F32)

    cq = (jnp.dot(xb, wcq_ref[...], preferred_element_type=F32)
          * (CROSS_HEAD_DIM ** -0.5)).astype(BF16)
    cg = jnp.dot(xb, wcg_ref[...], preferred_element_type=F32)
    outs = []
    for h in range(CROSS_HEADS):
        cols = slice(h * CROSS_HEAD_DIM, (h + 1) * CROSS_HEAD_DIM)
        mk = kv_ref[:, cols]
        mv = kv_ref[:, CROSS_WIDTH + h * CROSS_HEAD_DIM:CROSS_WIDTH + (h + 1) * CROSS_HEAD_DIM]
        s = lax.dot_general(cq[:, cols], mk, _NT, preferred_element_type=F32)
        e = jnp.exp(s - jnp.max(s, axis=-1, keepdims=True))
        p = e * (1.0 / jnp.sum(e, axis=-1, keepdims=True))
        outs.append(jnp.dot(p.astype(BF16), mv, preferred_element_type=F32))
    yc = jnp.concatenate(outs, axis=1) * _silu(cg)
    merged = merged + gate(2) * jnp.dot(yc.astype(BF16), wbc_ref[...],
                                        preferred_element_type=F32)

    out = jnp.dot(merged.astype(BF16), wo_ref[...], preferred_element_type=F32)
    r = DEEPNORM_ALPHA * x_ref[...] + out
    mu = jnp.mean(r, axis=-1, keepdims=True)
    d = r - mu
    var = jnp.mean(d * d, axis=-1, keepdims=True)
    xn = d * lax.rsqrt(var + NORM_EPS) * lng_ref[...] + lnb_ref[...]
    xo_ref[...] = xn
    xbo_ref[...] = xn.astype(BF16)


def _tail(x, xb, yf, yb, yd, kv, wz, wcq, wcg, wgl, wbs, wbd, wbc, wo, nw, gb, lng, lnb,
          seq, tm):
    T = x.shape[0]
    tm = min(tm, seq)
    tps = seq // tm
    mem_tokens = kv.shape[0] // (T // seq)
    row = lambda w: pl.BlockSpec((tm, w), lambda i: (i, 0))
    const = lambda a: pl.BlockSpec(a.shape, lambda i: (0, 0), pipeline_mode=pl.Buffered(1))
    return pl.pallas_call(
        _tail_kernel,
        grid=(T // tm,),
        in_specs=[row(D_MODEL), row(D_MODEL), row(SSD_INNER), row(SSD_INNER), row(DIFF_WIDTH),
                  pl.BlockSpec((mem_tokens, 2 * CROSS_WIDTH), lambda i: (i // tps, 0)),
                  const(wz), const(wcq), const(wcg), const(wgl), const(wbs), const(wbd),
                  const(wbc), const(wo), const(nw), const(gb), const(lng), const(lnb)],
        out_specs=[row(D_MODEL), row(D_MODEL)],
        out_shape=[jax.ShapeDtypeStruct((T, D_MODEL), F32),
                   jax.ShapeDtypeStruct((T, D_MODEL), BF16)],
        compiler_params=_cparams(("parallel",)),
        name="tail",
    )(x, xb, yf, yb, yd, kv, wz, wcq, wcg, wgl, wbs, wbd, wbc, wo, nw, gb, lng, lnb)


def _in_splits():
    sizes = [SSD_INNER, SSD_CONV_CH, 2 * SSD_HEADS, DIFF_WIDTH, DIFF_WIDTH, DIFF_WIDTH,
             DIFF_WIDTH, CROSS_WIDTH, CROSS_WIDTH, N_BRANCH * D_MODEL]
    offs = np.concatenate([[0], np.cumsum(sizes)])
    return [(int(offs[i]), int(offs[i + 1])) for i in range(len(sizes))]


def _rope_perm():
    half = DIFF_HEAD_DIM // 2
    perm = np.zeros(DIFF_WIDTH, np.int32)
    for h in range(DIFF_HEADS):
        for l in range(LANES):
            hf, c, j = l // 64, (l % 64) // half, l % half
            perm[h * LANES + l] = h * LANES + c * DIFF_HEAD_DIM + hf * half + j
    return perm


def _dt_placement():
    cols = np.zeros(2 * SSD_HEADS, np.int32)
    for d in range(2):
        for g in range(SSD_GROUPS):
            for r in range(SSD_HEADS_PER_GROUP):
                cols[d * SSD_HEADS + g * SSD_HEADS_PER_GROUP + r] = (
                    g * LANES + d * SSD_HEADS_PER_GROUP + r)
    return cols


def _per_group_lanes(v):
    out = jnp.zeros((SSD_GROUPS * LANES,), F32).at[_dt_placement()].set(v.reshape(-1).astype(F32))
    return out.reshape(SSD_GROUPS, 1, LANES)


def kernel(x, mem, positions, w_in, conv_w, conv_b, dt_bias, a_log, d_skip, ssd_norm_w,
           diff_lam, diff_norm_w, w_mem_kv, w_br_ssd, w_br_diff, w_br_cross, gate_b, w_out,
           ln_g, ln_b):
    batch, seq, _ = x.shape
    T = batch * seq
    mem_tokens = mem.shape[1]
    (sz, sxbc, sdt, sdq, sdk, sdv, sdg, scq, scg, sgl) = _in_splits()
    perm = _rope_perm()
    dt_cols = _dt_placement()

    inv = 1.0 / (ROPE_THETA ** (jnp.arange(0, DIFF_HEAD_DIM, 2, dtype=F32) / DIFF_HEAD_DIM))
    ang = positions.astype(F32)[..., None] * inv
    cos, sin = jnp.cos(ang), jnp.sin(ang)
    cos_t = jnp.concatenate([cos, cos, cos, cos], axis=-1).reshape(T, LANES)
    sin_t = jnp.concatenate([-sin, -sin, sin, sin], axis=-1).reshape(T, LANES)

    xf = x.reshape(T, D_MODEL)
    xb = xf.astype(BF16)
    memb = mem.reshape(batch * mem_tokens, D_MODEL).astype(BF16)

    for layer in range(DEPTH):
        lambda_init = 0.8 - 0.6 * math.exp(-0.3 * layer)
        w = w_in[layer]
        wb = lambda s: w[:, s[0]:s[1]].astype(BF16)
        w_q = wb(sdq)[:, perm]
        w_k = wb(sdk)[:, perm]
        w_vt = wb(sdv).T
        w_dt = jnp.zeros((D_MODEL, SSD_GROUPS * LANES), BF16).at[:, dt_cols].set(wb(sdt))
        conv_w8 = jnp.zeros((8, SSD_CONV_CH), F32).at[:SSD_CONV].set(conv_w[layer])

        xc = _proj_conv(xb, wb(sxbc), conv_w8, conv_b[layer][None, :], seq, F32, 512, 1024)
        dt = _proj_plain(xb, w_dt, F32, 1024, 512)
        q = _proj_rope(xb, w_q, cos_t, sin_t, DIFF_HEAD_DIM ** -0.5 * LOG2E, 1024, 1024)
        k = _proj_rope(xb, w_k, cos_t, sin_t, 1.0, 1024, 1024)
        vt = _proj_transposed(xb, w_vt, batch, seq, 1024, 1024)
        dg = _proj_plain(xb, wb(sdg), F32, 1024, 1024)
        kv = _proj_plain(memb, w_mem_kv[layer].astype(BF16), BF16, 1024, 1024)

        yf, yb = _ssd(xc, dt, _per_group_lanes(dt_bias[layer]), _per_group_lanes(a_log[layer]),
                      jnp.repeat(d_skip[layer].astype(F32), SSD_HEAD_DIM)[None, :],
                      batch, seq, 1024)

        lq = diff_lam[layer].astype(F32)
        lam = (jnp.exp(jnp.sum(lq[0] * lq[1])) - jnp.exp(jnp.sum(lq[2] * lq[3]))
               + lambda_init).reshape(1, 1)
        yd = _diff_attention(lam, q, k, vt, dg, diff_norm_w[layer].astype(F32)[None, :],
                             batch, seq, 1.0 - lambda_init, 512, 512)

        xf, xb = _tail(xf, xb, yf, yb, yd, kv,
                       wb(sz), wb(scq), wb(scg), wb(sgl),
                       w_br_ssd[layer].astype(BF16), w_br_diff[layer].astype(BF16),
                       w_br_cross[layer].astype(BF16), w_out[layer].astype(BF16),
                       ssd_norm_w[layer].astype(F32)[None, :],
                       gate_b[layer].astype(F32).reshape(1, N_BRANCH * D_MODEL),
                       ln_g[layer].astype(F32)[None, :], ln_b[layer].astype(F32)[None, :],
                       seq, 256)

    return xf.reshape(batch, seq, D_MODEL).astype(x.dtype)
```

```python
import functools
import math

import numpy as np
import jax
import jax.numpy as jnp
from jax import lax
from jax.experimental import pallas as pl
from jax.experimental.pallas import tpu as pltpu

F32 = jnp.float32
BF16 = jnp.bfloat16

D_MODEL = 1024
DEPTH = 2
SSD_INNER = 2048
SSD_HEAD_DIM = 64
SSD_HEADS = 32
SSD_GROUPS = 4
SSD_HEADS_PER_GROUP = SSD_HEADS // SSD_GROUPS
SSD_STATE = 128
SSD_CONV = 5
SSD_CHUNK = 128
SSD_CONV_CH = SSD_INNER + 2 * SSD_GROUPS * SSD_STATE
DIFF_HEAD_DIM = 64
DIFF_HEADS = 8
DIFF_WIDTH = 1024
ROPE_THETA = 10000.0
CROSS_HEADS = 4
CROSS_HEAD_DIM = 256
CROSS_WIDTH = 1024
N_BRANCH = 3
DEEPNORM_ALPHA = (2 * DEPTH) ** 0.25
NORM_EPS = 1e-5

LANES = 128
HALO = 16
ONES_ROWS = 16
LOG2E = 1.4426950408889634
VMEM_LIMIT = 56 * 1024 * 1024

_NT = (((1,), (1,)), ((), ()))


def _sigmoid(v):
    return 1.0 / (1.0 + jnp.exp(-v))


def _silu(v):
    return v * _sigmoid(v)


def _cparams(sem):
    return pltpu.CompilerParams(dimension_semantics=sem, vmem_limit_bytes=VMEM_LIMIT)


def _proj_plain_kernel(x_ref, w_ref, o_ref):
    o_ref[...] = jnp.dot(x_ref[...], w_ref[...],
                         preferred_element_type=F32).astype(o_ref.dtype)


def _proj_plain(xb, w, out_dtype, tm, tn):
    T, K = xb.shape
    N = w.shape[1]
    tm, tn = min(tm, T), min(tn, N)
    return pl.pallas_call(
        _proj_plain_kernel,
        grid=(N // tn, T // tm),
        in_specs=[pl.BlockSpec((tm, K), lambda j, i: (i, 0)),
                  pl.BlockSpec((K, tn), lambda j, i: (0, j))],
        out_specs=pl.BlockSpec((tm, tn), lambda j, i: (i, j)),
        out_shape=jax.ShapeDtypeStruct((T, N), out_dtype),
        compiler_params=_cparams(("parallel", "parallel")),
        name="proj_plain",
    )(xb, w)


def _proj_attn_kernel(x_ref, wq_ref, wk_ref, wvt_ref, wdg_ref, wdt_ref, cos_ref, sin_ref,
                      q_ref, k_ref, vt_ref, dg_ref, dt_ref, *, q_scale):
    x = x_ref[...]
    c = cos_ref[...]
    s = sin_ref[...]

    def rope(w_ref, o_ref, scale):
        acc = jnp.dot(x, w_ref[...], preferred_element_type=F32)
        for j in range(acc.shape[1] // LANES):
            t = acc[:, j * LANES:(j + 1) * LANES]
            o = t * c + pltpu.roll(t, LANES // 2, axis=1) * s
            if scale != 1.0:
                o = o * scale
            o_ref[:, j * LANES:(j + 1) * LANES] = o.astype(o_ref.dtype)

    rope(wq_ref, q_ref, q_scale)
    rope(wk_ref, k_ref, 1.0)
    vt_ref[...] = lax.dot_general(wvt_ref[...], x, _NT,
                                  preferred_element_type=F32).astype(vt_ref.dtype)
    dg_ref[...] = jnp.dot(x, wdg_ref[...], preferred_element_type=F32)
    dt_ref[...] = jnp.dot(x, wdt_ref[...], preferred_element_type=F32)


def _proj_attn(xb, w_q, w_k, w_vt, w_dg, w_dt, cos_t, sin_t, q_scale, batch, seq, tm):
    T, K = xb.shape
    tm = min(tm, seq)
    ns = seq // tm
    row = lambda n: pl.BlockSpec((tm, n), lambda i: (i, 0))
    const = lambda a: pl.BlockSpec(a.shape, lambda i: (0, 0), pipeline_mode=pl.Buffered(1))
    n_v, n_dt = w_vt.shape[0], w_dt.shape[1]
    return pl.pallas_call(
        functools.partial(_proj_attn_kernel, q_scale=q_scale),
        grid=(T // tm,),
        in_specs=[row(K), const(w_q), const(w_k), const(w_vt), const(w_dg), const(w_dt),
                  row(LANES), row(LANES)],
        out_specs=[row(DIFF_WIDTH), row(DIFF_WIDTH),
                   pl.BlockSpec((None, n_v, tm), lambda i: (i // ns, 0, i % ns)),
                   row(DIFF_WIDTH), row(n_dt)],
        out_shape=[jax.ShapeDtypeStruct((T, DIFF_WIDTH), BF16),
                   jax.ShapeDtypeStruct((T, DIFF_WIDTH), BF16),
                   jax.ShapeDtypeStruct((batch, n_v, seq), BF16),
                   jax.ShapeDtypeStruct((T, DIFF_WIDTH), F32),
                   jax.ShapeDtypeStruct((T, n_dt), F32)],
        compiler_params=_cparams(("parallel",)),
        name="proj_attn",
    )(xb, w_q, w_k, w_vt, w_dg, w_dt, cos_t, sin_t)


def _proj_conv_kernel(xp_ref, x_ref, xn_ref, w_ref, cw_ref, cb_ref, o_ref, xcat_ref,
                      *, tm, tiles_per_seq):
    i = pl.program_id(1)
    pos = i % tiles_per_seq
    xp = xp_ref[...]
    xn = xn_ref[...]
    xcat_ref[0:HALO, :] = jnp.where(pos == 0, jnp.zeros_like(xp), xp)
    xcat_ref[HALO:HALO + tm, :] = x_ref[...]
    xcat_ref[HALO + tm:, :] = jnp.where(pos == tiles_per_seq - 1, jnp.zeros_like(xn), xn)
    acc = jnp.dot(xcat_ref[...], w_ref[...], preferred_element_type=F32)
    rows = acc.shape[0]
    pad = SSD_CONV // 2
    out = cb_ref[...]
    for j in range(SSD_CONV):
        shifted = acc if j == pad else pltpu.roll(acc, (pad - j) % rows, axis=0)
        out = out + cw_ref[j:j + 1, :] * shifted[HALO:HALO + tm, :]
    o_ref[...] = _silu(out).astype(o_ref.dtype)


def _proj_conv(xb, w, conv_w8, conv_b, seq, out_dtype, tm, tn):
    T, K = xb.shape
    N = w.shape[1]
    tm, tn = min(tm, seq), min(tn, N)
    tps = seq // tm
    hb = tm // HALO
    nhb = T // HALO
    return pl.pallas_call(
        functools.partial(_proj_conv_kernel, tm=tm, tiles_per_seq=tps),
        grid=(N // tn, T // tm),
        in_specs=[pl.BlockSpec((HALO, K), lambda j, i: (jnp.maximum(i * hb - 1, 0), 0)),
                  pl.BlockSpec((tm, K), lambda j, i: (i, 0)),
                  pl.BlockSpec((HALO, K), lambda j, i: (jnp.minimum((i + 1) * hb, nhb - 1), 0)),
                  pl.BlockSpec((K, tn), lambda j, i: (0, j)),
                  pl.BlockSpec((8, tn), lambda j, i: (0, j)),
                  pl.BlockSpec((1, tn), lambda j, i: (0, j))],
        out_specs=pl.BlockSpec((tm, tn), lambda j, i: (i, j)),
        out_shape=jax.ShapeDtypeStruct((T, N), out_dtype),
        scratch_shapes=[pltpu.VMEM((tm + 2 * HALO, K), BF16)],
        compiler_params=_cparams(("parallel", "parallel")),
        name="proj_conv",
    )(xb, xb, xb, w, conv_w8, conv_b)


def _softplus(v):
    return jnp.maximum(v, 0.0) + jnp.log(1.0 + jnp.exp(-jnp.abs(v)))


def _split3(v):
    hi = v.astype(BF16)
    r1 = v - hi.astype(F32)
    mid = r1.astype(BF16)
    lo = (r1 - mid.astype(F32)).astype(BF16)
    return hi, mid, lo


def _ssd_keep(forward):
    qi = lax.broadcasted_iota(jnp.int32, (SSD_CHUNK, SSD_CHUNK), 0)
    si = lax.broadcasted_iota(jnp.int32, (SSD_CHUNK, SSD_CHUNK), 1)
    return si <= qi if forward else si >= qi


def _ssd_prepare(bm, cm, dt_raw, bias, a_neg, forward, pre_ref):
    tri = jnp.where(_ssd_keep(forward), 1.0, 0.0).astype(BF16)
    dtc = _softplus(dt_raw + bias)
    a = dtc * (a_neg * LOG2E)
    hi, mid, lo = _split3(a)
    cum = (jnp.dot(tri, hi, preferred_element_type=F32)
           + jnp.dot(tri, mid, preferred_element_type=F32)
           + jnp.dot(tri, lo, preferred_element_type=F32))
    pre_ref[0] = cum
    pre_ref[1] = cum.T - jnp.log2(dtc.T)
    pre_ref[2] = lax.dot_general(cm.astype(BF16), bm.astype(BF16), _NT,
                                 preferred_element_type=F32)
    pre_ref[3] = bm.T


def _ssd_main(xs, cm, pre_ref, h_ref, y_ref, row0, lane0, forward, dskip):
    Q = SSD_CHUNK
    keep = _ssd_keep(forward)
    last = Q - 1 if forward else 0
    cum = pre_ref[0]
    row_t = pre_ref[1]
    gm = pre_ref[2]
    bt = pre_ref[3]
    lane = lax.broadcasted_iota(jnp.int32, (Q, LANES), 1)
    first_half = lane < SSD_HEAD_DIM

    for pair in range(SSD_HEADS_PER_GROUP // 2):
        cols = slice(pair * LANES, (pair + 1) * LANES)
        xs_pair = xs[:, cols]
        h_pair = h_ref[:, cols]
        y_pair = None
        s_pair = None
        decs = []
        for e in range(2):
            ln = lane0 + 2 * pair + e
            hmask = first_half if e == 0 else jnp.logical_not(first_half)
            colb = jnp.broadcast_to(cum[:, ln:ln + 1], (Q, Q))
            rowb = jnp.broadcast_to(row_t[ln:ln + 1, :], (Q, Q))
            decay = jnp.where(keep, jnp.exp2(colb - rowb), 0.0)
            m1 = (gm * decay).astype(BF16)
            m2 = (cm * jnp.exp2(colb)).astype(BF16)
            lhs = jnp.concatenate([m1, m2], axis=1)
            xr = jnp.where(hmask, xs_pair, 0.0).astype(BF16)
            hr = jnp.where(hmask, h_pair, 0.0).astype(BF16)
            rhs = jnp.concatenate([xr, hr], axis=0)
            yc = jnp.dot(lhs, rhs, preferred_element_type=F32)
            y_pair = yc if y_pair is None else y_pair + yc
            tot = colb[last:last + 1, :]
            wb = (bt * jnp.exp2(tot - rowb)).astype(BF16)
            sc = jnp.dot(wb, xr, preferred_element_type=F32)
            s_pair = sc if s_pair is None else s_pair + sc
            decs.append(jnp.exp2(tot))
        dec = jnp.where(first_half[0:1, :], decs[0], decs[1])
        h_ref[:, cols] = h_pair * dec + s_pair
        if dskip is not None:
            y_pair = y_pair + dskip[:, cols] * xs_pair
        y_ref[pl.ds(row0, Q), cols] = y_pair


def _ssd_kernel(xsf_ref, bf_ref, cf_ref, dtf_ref, xsb_ref, bb_ref, cb_ref, dtb_ref,
                bias_ref, alog_ref, dsk_ref, yf_ref, yb_ref, hf_ref, hb_ref, pre_ref, *, nck):
    @pl.when(pl.program_id(2) == 0)
    def _():
        hf_ref[...] = jnp.zeros_like(hf_ref)
        hb_ref[...] = jnp.zeros_like(hb_ref)

    bias = bias_ref[...]
    a_neg = -jnp.exp(alog_ref[...])
    dsk = dsk_ref[...]
    Q = SSD_CHUNK

    def rows(j):
        if isinstance(j, int):
            return j * Q, (nck - 1 - j) * Q
        return pl.multiple_of(j * Q, Q), pl.multiple_of((nck - 1 - j) * Q, Q)

    def prepare(j, slot):
        rf, rb = rows(j)
        _ssd_prepare(bf_ref[pl.ds(rf, Q), :], cf_ref[pl.ds(rf, Q), :], dtf_ref[pl.ds(rf, Q), :],
                     bias, a_neg, True, pre_ref.at[slot, 0])
        _ssd_prepare(bb_ref[pl.ds(rb, Q), :], cb_ref[pl.ds(rb, Q), :], dtb_ref[pl.ds(rb, Q), :],
                     bias, a_neg, False, pre_ref.at[slot, 1])

    def main(j, slot):
        rf, rb = rows(j)
        _ssd_main(xsf_ref[pl.ds(rf, Q), :], cf_ref[pl.ds(rf, Q), :], pre_ref.at[slot, 0],
                  hf_ref, yf_ref, rf, 0, True, dsk)
        _ssd_main(xsb_ref[pl.ds(rb, Q), :], cb_ref[pl.ds(rb, Q), :], pre_ref.at[slot, 1],
                  hb_ref, yb_ref, rb, SSD_HEADS_PER_GROUP, False, None)

    prepare(0, 0)

    def body(t, carry):
        prepare(2 * t + 1, 1)
        main(2 * t, 0)
        prepare(jnp.minimum(2 * t + 2, nck - 1), 0)
        main(2 * t + 1, 1)
        return carry

    lax.fori_loop(0, nck // 2, body, 0)


def _ssd(xc, dt, bias, alog, dskip, batch, seq, rb):
    T = xc.shape[0]
    rb = min(rb, seq)
    nb = seq // rb
    nck = rb // SSD_CHUNK
    assert nck % 2 == 0 and nb * rb == seq
    gw = SSD_HEADS_PER_GROUP * SSD_HEAD_DIM
    b_off = SSD_INNER // LANES
    c_off = b_off + SSD_GROUPS

    def fwd(col):
        return lambda b, g, i: (b * nb + i, col(g))

    def bwd(col):
        return lambda b, g, i: (b * nb + nb - 1 - i, col(g))

    def specs(mk):
        return [pl.BlockSpec((rb, gw), mk(lambda g: g)),
                pl.BlockSpec((rb, LANES), mk(lambda g: b_off + g)),
                pl.BlockSpec((rb, LANES), mk(lambda g: c_off + g)),
                pl.BlockSpec((rb, LANES), mk(lambda g: g))]

    out_sds = jax.ShapeDtypeStruct((T, SSD_INNER), F32)
    return pl.pallas_call(
        functools.partial(_ssd_kernel, nck=nck),
        grid=(batch, SSD_GROUPS, nb),
        in_specs=specs(fwd) + specs(bwd) + [
            pl.BlockSpec((None, 1, LANES), lambda b, g, i: (g, 0, 0)),
            pl.BlockSpec((None, 1, LANES), lambda b, g, i: (g, 0, 0)),
            pl.BlockSpec((1, gw), lambda b, g, i: (0, g))],
        out_specs=[pl.BlockSpec((rb, gw), fwd(lambda g: g)),
                   pl.BlockSpec((rb, gw), bwd(lambda g: g))],
        out_shape=[out_sds, out_sds],
        scratch_shapes=[pltpu.VMEM((SSD_STATE, gw), F32), pltpu.VMEM((SSD_STATE, gw), F32),
                        pltpu.VMEM((2, 2, 4, SSD_CHUNK, LANES), F32)],
        compiler_params=_cparams(("parallel", "parallel", "arbitrary")),
        name="ssd_scan",
    )(xc, xc, xc, dt, xc, xc, xc, dt, bias, alog, dskip)


def _attn_kernel(lam_ref, q_ref, k_ref, vt_ref, dg_ref, nw_ref, o_ref,
                 qm_ref, s_ref, bm_ref, m_ref, acc_ref, *, tk, nk, out_scale):
    q = q_ref[...]
    lane = lax.broadcasted_iota(jnp.int32, q.shape, 1)
    map0 = (lane % SSD_HEAD_DIM) < (DIFF_HEAD_DIM // 2)
    zero = jnp.zeros_like(q)
    qm_ref[0] = jnp.where(map0, q, zero)
    qm_ref[1] = jnp.where(map0, zero, q)
    m_ref[...] = jnp.full_like(m_ref, -jnp.inf)
    acc_ref[...] = jnp.zeros_like(acc_ref)
    ones = jnp.ones((ONES_ROWS, tk), BF16)
    dv = 2 * DIFF_HEAD_DIM

    def key_start(blk):
        return blk * tk if isinstance(blk, int) else pl.multiple_of(blk * tk, tk)

    def scores(blk, slot):
        k = k_ref[pl.ds(key_start(blk), tk), :]
        for c in range(2):
            s = lax.dot_general(k, qm_ref[c], _NT, preferred_element_type=F32)
            s_ref[slot, c] = s
            bm_ref[slot, c] = jnp.max(s, axis=0, keepdims=True)

    def softmax_pv(blk, slot):
        vt = vt_ref[:, pl.ds(key_start(blk), tk)]
        lhs = jnp.concatenate([vt, ones], axis=0)
        for c in range(2):
            s = s_ref[slot, c]
            m_old = m_ref[c]
            m_new = jnp.maximum(m_old, bm_ref[slot, c])
            alpha = jnp.exp2(m_old - m_new)
            p = jnp.exp2(s - m_new).astype(BF16)
            acc_ref[c] = acc_ref[c] * alpha + jnp.dot(lhs, p, preferred_element_type=F32)
            m_ref[c] = m_new

    scores(0, 0)

    def body(j, carry):
        scores(2 * j + 1, 1)
        softmax_pv(2 * j, 0)
        scores(2 * j + 2, 0)
        softmax_pv(2 * j + 1, 1)
        return carry

    lax.fori_loop(0, nk // 2 - 1, body, 0)
    scores(nk - 1, 1)
    softmax_pv(nk - 2, 0)
    softmax_pv(nk - 1, 1)

    lam = lam_ref[0, 0]
    o = (acc_ref[0, :dv, :] * (1.0 / acc_ref[0, dv:dv + 1, :])
         - lam * (acc_ref[1, :dv, :] * (1.0 / acc_ref[1, dv:dv + 1, :])))
    ms = jnp.mean(o * o, axis=0, keepdims=True)
    o = o * lax.rsqrt(ms + NORM_EPS)
    ot = o.T * (nw_ref[...] * out_scale)
    o_ref[...] = (ot * _silu(dg_ref[...])).astype(o_ref.dtype)


def _diff_attention(lam, q, k, vt, dg, norm_w, batch, seq, out_scale, tq, tk):
    T = q.shape[0]
    tq, tk = min(tq, seq), min(tk, seq // 2)
    nq = seq // tq
    nk = seq // tk
    assert nk % 2 == 0 and nk * tk == seq
    return pl.pallas_call(
        functools.partial(_attn_kernel, tk=tk, nk=nk, out_scale=out_scale),
        grid=(batch, DIFF_HEADS, nq),
        in_specs=[pl.BlockSpec(memory_space=pltpu.SMEM),
                  pl.BlockSpec((tq, LANES), lambda b, h, i: (b * nq + i, h)),
                  pl.BlockSpec((seq, LANES), lambda b, h, i: (b, h)),
                  pl.BlockSpec((None, LANES, seq), lambda b, h, i: (b, h, 0)),
                  pl.BlockSpec((tq, LANES), lambda b, h, i: (b * nq + i, h)),
                  pl.BlockSpec((1, LANES), lambda b, h, i: (0, 0))],
        out_specs=pl.BlockSpec((tq, LANES), lambda b, h, i: (b * nq + i, h)),
        out_shape=jax.ShapeDtypeStruct((T, DIFF_WIDTH), BF16),
        scratch_shapes=[pltpu.VMEM((2, tq, LANES), BF16),
                        pltpu.VMEM((2, 2, tk, tq), F32),
                        pltpu.VMEM((2, 2, 1, tq), F32),
                        pltpu.VMEM((2, 1, tq), F32),
                        pltpu.VMEM((2, LANES + ONES_ROWS, tq), F32)],
        compiler_params=_cparams(("parallel", "parallel", "parallel")),
        name="diff_attention",
    )(lam, q, k, vt, dg, norm_w)


def _tail_kernel(x_ref, xb_ref, yf_ref, yb_ref, yd_ref, kv_ref,
                 wz_ref, wcq_ref, wcg_ref, wgl_ref, wbs_ref, wbd_ref, wbc_ref, wo_ref,
                 nw_ref, gb_ref, lng_ref, lnb_ref, xo_ref, xbo_ref):
    xb = xb_ref[...]

    def gate(idx):
        cols = slice(idx * D_MODEL, (idx + 1) * D_MODEL)
        gl = jnp.dot(xb, wgl_ref[:, cols], preferred_element_type=F32) + gb_ref[:, cols]
        return _sigmoid(gl)

    z = jnp.dot(xb, wz_ref[...], preferred_element_type=F32)
    y = (yf_ref[...] + yb_ref[...]) * _silu(z)
    y = y * lax.rsqrt(jnp.mean(y * y, axis=-1, keepdims=True) + NORM_EPS) * nw_ref[...]
    merged = gate(0) * jnp.dot(y.astype(BF16), wbs_ref[...], preferred_element_type=F32)

    merged = merged + gate(1) * jnp.dot(yd_ref[...], wbd_ref[...], preferred_element_type=F32)

    cq = (jnp.dot(xb, wcq_ref[...], preferred_element_type=F32)
          * (CROSS_HEAD_DIM ** -0.5)).astype(BF16)
    cg = jnp.dot(xb, wcg_ref[...], preferred_element_type=F32)
    outs = []
    for h in range(CROSS_HEADS):
        cols = slice(h * CROSS_HEAD_DIM, (h + 1) * CROSS_HEAD_DIM)
        mk = kv_ref[:, cols]
        mv = kv_ref[:, CROSS_WIDTH + h * CROSS_HEAD_DIM:CROSS_WIDTH + (h + 1) * CROSS_HEAD_DIM]
        s = lax.dot_general(cq[:, cols], mk, _NT, preferred_element_type=F32)
        e = jnp.exp(s - jnp.max(s, axis=-1, keepdims=True))
        p = e * (1.0 / jnp.sum(e, axis=-1, keepdims=True))
        outs.append(jnp.dot(p.astype(BF16), mv, preferred_element_type=F32))
    yc = jnp.concatenate(outs, axis=1) * _silu(cg)
    merged = merged + gate(2) * jnp.dot(yc.astype(BF16), wbc_ref[...],
                                        preferred_element_type=F32)

    out = jnp.dot(merged.astype(BF16), wo_ref[...], preferred_element_type=F32)
    r = DEEPNORM_ALPHA * x_ref[...] + out
    mu = jnp.mean(r, axis=-1, keepdims=True)
    d = r - mu
    var = jnp.mean(d * d, axis=-1, keepdims=True)
    xn = d * lax.rsqrt(var + NORM_EPS) * lng_ref[...] + lnb_ref[...]
    xo_ref[...] = xn
    xbo_ref[...] = xn.astype(BF16)


def _tail(x, xb, yf, yb, yd, kv, wz, wcq, wcg, wgl, wbs, wbd, wbc, wo, nw, gb, lng, lnb,
          seq, tm):
    T = x.shape[0]
    tm = min(tm, seq)
    tps = seq // tm
    mem_tokens = kv.shape[0] // (T // seq)
    row = lambda w: pl.BlockSpec((tm, w), lambda i: (i, 0))
    const = lambda a: pl.BlockSpec(a.shape, lambda i: (0, 0), pipeline_mode=pl.Buffered(1))
    return pl.pallas_call(
        _tail_kernel,
        grid=(T // tm,),
        in_specs=[row(D_MODEL), row(D_MODEL), row(SSD_INNER), row(SSD_INNER), row(DIFF_WIDTH),
                  pl.BlockSpec((mem_tokens, 2 * CROSS_WIDTH), lambda i: (i // tps, 0)),
                  const(wz), const(wcq), const(wcg), const(wgl), const(wbs), const(wbd),
                  const(wbc), const(wo), const(nw), const(gb), const(lng), const(lnb)],
        out_specs=[row(D_MODEL), row(D_MODEL)],
        out_shape=[jax.ShapeDtypeStruct((T, D_MODEL), F32),
                   jax.ShapeDtypeStruct((T, D_MODEL), BF16)],
        compiler_params=_cparams(("parallel",)),
        name="tail",
    )(x, xb, yf, yb, yd, kv, wz, wcq, wcg, wgl, wbs, wbd, wbc, wo, nw, gb, lng, lnb)


def _in_splits():
    sizes = [SSD_INNER, SSD_CONV_CH, 2 * SSD_HEADS, DIFF_WIDTH, DIFF_WIDTH, DIFF_WIDTH,
             DIFF_WIDTH, CROSS_WIDTH, CROSS_WIDTH, N_BRANCH * D_MODEL]
    offs = np.concatenate([[0], np.cumsum(sizes)])
    return [(int(offs[i]), int(offs[i + 1])) for i in range(len(sizes))]


def _rope_perm():
    half = DIFF_HEAD_DIM // 2
    perm = np.zeros(DIFF_WIDTH, np.int32)
    for h in range(DIFF_HEADS):
        for l in range(LANES):
            hf, c, j = l // 64, (l % 64) // half, l % half
            perm[h * LANES + l] = h * LANES + c * DIFF_HEAD_DIM + hf * half + j
    return perm


def _dt_placement():
    cols = np.zeros(2 * SSD_HEADS, np.int32)
    for d in range(2):
        for g in range(SSD_GROUPS):
            for r in range(SSD_HEADS_PER_GROUP):
                cols[d * SSD_HEADS + g * SSD_HEADS_PER_GROUP + r] = (
                    g * LANES + d * SSD_HEADS_PER_GROUP + r)
    return cols


def _per_group_lanes(v):
    out = jnp.zeros((SSD_GROUPS * LANES,), F32).at[_dt_placement()].set(v.reshape(-1).astype(F32))
    return out.reshape(SSD_GROUPS, 1, LANES)


def kernel(x, mem, positions, w_in, conv_w, conv_b, dt_bias, a_log, d_skip, ssd_norm_w,
           diff_lam, diff_norm_w, w_mem_kv, w_br_ssd, w_br_diff, w_br_cross, gate_b, w_out,
           ln_g, ln_b):
    batch, seq, _ = x.shape
    T = batch * seq
    mem_tokens = mem.shape[1]
    (sz, sxbc, sdt, sdq, sdk, sdv, sdg, scq, scg, sgl) = _in_splits()
    perm = _rope_perm()
    dt_cols = _dt_placement()

    inv = 1.0 / (ROPE_THETA ** (jnp.arange(0, DIFF_HEAD_DIM, 2, dtype=F32) / DIFF_HEAD_DIM))
    ang = positions.astype(F32)[..., None] * inv
    cos, sin = jnp.cos(ang), jnp.sin(ang)
    cos_t = jnp.concatenate([cos, cos, cos, cos], axis=-1).reshape(T, LANES)
    sin_t = jnp.concatenate([-sin, -sin, sin, sin], axis=-1).reshape(T, LANES)

    xf = x.reshape(T, D_MODEL)
    xb = xf.astype(BF16)
    memb = mem.reshape(batch * mem_tokens, D_MODEL).astype(BF16)

    for layer in range(DEPTH):
        lambda_init = 0.8 - 0.6 * math.exp(-0.3 * layer)
        w = w_in[layer]
        wb = lambda s: w[:, s[0]:s[1]].astype(BF16)
        w_q = wb(sdq)[:, perm]
        w_k = wb(sdk)[:, perm]
        w_vt = wb(sdv).T
        w_dt = jnp.zeros((D_MODEL, SSD_GROUPS * LANES), BF16).at[:, dt_cols].set(wb(sdt))
        conv_w8 = jnp.zeros((8, SSD_CONV_CH), F32).at[:SSD_CONV].set(conv_w[layer])

        xc = _proj_conv(xb, wb(sxbc), conv_w8, conv_b[layer][None, :], seq, F32, 512, 1024)
        q, k, vt, dg, dt = _proj_attn(xb, w_q, w_k, w_vt, wb(sdg), w_dt, cos_t, sin_t,
                                      DIFF_HEAD_DIM ** -0.5 * LOG2E, batch, seq, 512)
        kv = _proj_plain(memb, w_mem_kv[layer].astype(BF16), BF16, 1024, 1024)

        yf, yb = _ssd(xc, dt, _per_group_lanes(dt_bias[layer]), _per_group_lanes(a_log[layer]),
                      jnp.repeat(d_skip[layer].astype(F32), SSD_HEAD_DIM)[None, :],
                      batch, seq, 1024)

        lq = diff_lam[layer].astype(F32)
        lam = (jnp.exp(jnp.sum(lq[0] * lq[1])) - jnp.exp(jnp.sum(lq[2] * lq[3]))
               + lambda_init).reshape(1, 1)
        yd = _diff_attention(lam, q, k, vt, dg, diff_norm_w[layer].astype(F32)[None, :],
                             batch, seq, 1.0 - lambda_init, 512, 512)

        xf, xb = _tail(xf, xb, yf, yb, yd, kv,
                       wb(sz), wb(scq), wb(scg), wb(sgl),
                       w_br_ssd[layer].astype(BF16), w_br_diff[layer].astype(BF16),
                       w_br_cross[layer].astype(BF16), w_out[layer].astype(BF16),
                       ssd_norm_w[layer].astype(F32)[None, :],
                       gate_b[layer].astype(F32).reshape(1, N_BRANCH * D_MODEL),
                       ln_g[layer].astype(F32)[None, :], ln_b[layer].astype(F32)[None, :],
                       seq, 256)

    return xf.reshape(batch, seq, D_MODEL).astype(x.dtype)
```

```python
import functools
import math

import numpy as np
import jax
import jax.numpy as jnp
from jax import lax
from jax.experimental import pallas as pl
from jax.experimental.pallas import tpu as pltpu

F32 = jnp.float32
BF16 = jnp.bfloat16

D_MODEL = 1024
DEPTH = 2
SSD_INNER = 2048
SSD_HEAD_DIM = 64
SSD_HEADS = 32
SSD_GROUPS = 4
SSD_HEADS_PER_GROUP = SSD_HEADS // SSD_GROUPS
SSD_STATE = 128
SSD_CONV = 5
SSD_CHUNK = 128
SSD_CONV_CH = SSD_INNER + 2 * SSD_GROUPS * SSD_STATE
DIFF_HEAD_DIM = 64
DIFF_HEADS = 8
DIFF_WIDTH = 1024
ROPE_THETA = 10000.0
CROSS_HEADS = 4
CROSS_HEAD_DIM = 256
CROSS_WIDTH = 1024
N_BRANCH = 3
DEEPNORM_ALPHA = (2 * DEPTH) ** 0.25
NORM_EPS = 1e-5

LANES = 128
HALO = 16
ONES_ROWS = 16
PAIRS_PER_TRIP = 3
SSD_STEPS_PER_TRIP = 2
LOG2E = 1.4426950408889634
VMEM_LIMIT = 56 * 1024 * 1024

_NT = (((1,), (1,)), ((), ()))


def _sigmoid(v):
    return 1.0 / (1.0 + jnp.exp(-v))


def _silu(v):
    return v * _sigmoid(v)


def _cparams(sem):
    return pltpu.CompilerParams(dimension_semantics=sem, vmem_limit_bytes=VMEM_LIMIT)


def _proj_plain_kernel(x_ref, w_ref, o_ref):
    o_ref[...] = jnp.dot(x_ref[...], w_ref[...],
                         preferred_element_type=F32).astype(o_ref.dtype)


def _proj_plain(xb, w, out_dtype, tm, tn):
    T, K = xb.shape
    N = w.shape[1]
    tm, tn = min(tm, T), min(tn, N)
    return pl.pallas_call(
        _proj_plain_kernel,
        grid=(N // tn, T // tm),
        in_specs=[pl.BlockSpec((tm, K), lambda j, i: (i, 0)),
                  pl.BlockSpec((K, tn), lambda j, i: (0, j))],
        out_specs=pl.BlockSpec((tm, tn), lambda j, i: (i, j)),
        out_shape=jax.ShapeDtypeStruct((T, N), out_dtype),
        compiler_params=_cparams(("parallel", "parallel")),
        name="proj_plain",
    )(xb, w)


def _proj_attn_kernel(x_ref, wq_ref, wk_ref, wvt_ref, wdg_ref, wdt_ref, cos_ref, sin_ref,
                      q_ref, k_ref, vt_ref, dg_ref, dt_ref, *, q_scale):
    x = x_ref[...]
    c = cos_ref[...]
    s = sin_ref[...]

    def rope(w_ref, o_ref, scale):
        acc = jnp.dot(x, w_ref[...], preferred_element_type=F32)
        for j in range(acc.shape[1] // LANES):
            t = acc[:, j * LANES:(j + 1) * LANES]
            o = t * c + pltpu.roll(t, LANES // 2, axis=1) * s
            if scale != 1.0:
                o = o * scale
            o_ref[:, j * LANES:(j + 1) * LANES] = o.astype(o_ref.dtype)

    rope(wq_ref, q_ref, q_scale)
    rope(wk_ref, k_ref, 1.0)
    vt_ref[...] = lax.dot_general(wvt_ref[...], x, _NT,
                                  preferred_element_type=F32).astype(vt_ref.dtype)
    dg_ref[...] = jnp.dot(x, wdg_ref[...], preferred_element_type=F32)
    dt_ref[...] = jnp.dot(x, wdt_ref[...], preferred_element_type=F32)


def _proj_attn(xb, w_q, w_k, w_vt, w_dg, w_dt, cos_t, sin_t, q_scale, batch, seq, tm):
    T, K = xb.shape
    tm = min(tm, seq)
    ns = seq // tm
    row = lambda n: pl.BlockSpec((tm, n), lambda i: (i, 0))
    const = lambda a: pl.BlockSpec(a.shape, lambda i: (0, 0), pipeline_mode=pl.Buffered(1))
    n_v, n_dt = w_vt.shape[0], w_dt.shape[1]
    return pl.pallas_call(
        functools.partial(_proj_attn_kernel, q_scale=q_scale),
        grid=(T // tm,),
        in_specs=[row(K), const(w_q), const(w_k), const(w_vt), const(w_dg), const(w_dt),
                  row(LANES), row(LANES)],
        out_specs=[row(DIFF_WIDTH), row(DIFF_WIDTH),
                   pl.BlockSpec((None, n_v, tm), lambda i: (i // ns, 0, i % ns)),
                   row(DIFF_WIDTH), row(n_dt)],
        out_shape=[jax.ShapeDtypeStruct((T, DIFF_WIDTH), BF16),
                   jax.ShapeDtypeStruct((T, DIFF_WIDTH), BF16),
                   jax.ShapeDtypeStruct((batch, n_v, seq), BF16),
                   jax.ShapeDtypeStruct((T, DIFF_WIDTH), F32),
                   jax.ShapeDtypeStruct((T, n_dt), F32)],
        compiler_params=_cparams(("parallel",)),
        name="proj_attn",
    )(xb, w_q, w_k, w_vt, w_dg, w_dt, cos_t, sin_t)


def _proj_conv_kernel(xp_ref, x_ref, xn_ref, w_ref, cw_ref, cb_ref, o_ref, xcat_ref,
                      *, tm, tiles_per_seq):
    i = pl.program_id(1)
    pos = i % tiles_per_seq
    xp = xp_ref[...]
    xn = xn_ref[...]
    xcat_ref[0:HALO, :] = jnp.where(pos == 0, jnp.zeros_like(xp), xp)
    xcat_ref[HALO:HALO + tm, :] = x_ref[...]
    xcat_ref[HALO + tm:, :] = jnp.where(pos == tiles_per_seq - 1, jnp.zeros_like(xn), xn)
    acc = jnp.dot(xcat_ref[...], w_ref[...], preferred_element_type=F32)
    rows = acc.shape[0]
    pad = SSD_CONV // 2
    out = cb_ref[...]
    for j in range(SSD_CONV):
        shifted = acc if j == pad else pltpu.roll(acc, (pad - j) % rows, axis=0)
        out = out + cw_ref[j:j + 1, :] * shifted[HALO:HALO + tm, :]
    o_ref[...] = _silu(out).astype(o_ref.dtype)


def _proj_conv(xb, w, conv_w8, conv_b, seq, out_dtype, tm, tn):
    T, K = xb.shape
    N = w.shape[1]
    tm, tn = min(tm, seq), min(tn, N)
    tps = seq // tm
    hb = tm // HALO
    nhb = T // HALO
    return pl.pallas_call(
        functools.partial(_proj_conv_kernel, tm=tm, tiles_per_seq=tps),
        grid=(N // tn, T // tm),
        in_specs=[pl.BlockSpec((HALO, K), lambda j, i: (jnp.maximum(i * hb - 1, 0), 0)),
                  pl.BlockSpec((tm, K), lambda j, i: (i, 0)),
                  pl.BlockSpec((HALO, K), lambda j, i: (jnp.minimum((i + 1) * hb, nhb - 1), 0)),
                  pl.BlockSpec((K, tn), lambda j, i: (0, j)),
                  pl.BlockSpec((8, tn), lambda j, i: (0, j)),
                  pl.BlockSpec((1, tn), lambda j, i: (0, j))],
        out_specs=pl.BlockSpec((tm, tn), lambda j, i: (i, j)),
        out_shape=jax.ShapeDtypeStruct((T, N), out_dtype),
        scratch_shapes=[pltpu.VMEM((tm + 2 * HALO, K), BF16)],
        compiler_params=_cparams(("parallel", "parallel")),
        name="proj_conv",
    )(xb, xb, xb, w, conv_w8, conv_b)


def _softplus(v):
    return jnp.maximum(v, 0.0) + jnp.log(1.0 + jnp.exp(-jnp.abs(v)))


def _split3(v):
    hi = v.astype(BF16)
    r1 = v - hi.astype(F32)
    mid = r1.astype(BF16)
    lo = (r1 - mid.astype(F32)).astype(BF16)
    return hi, mid, lo


def _ssd_keep(forward):
    qi = lax.broadcasted_iota(jnp.int32, (SSD_CHUNK, SSD_CHUNK), 0)
    si = lax.broadcasted_iota(jnp.int32, (SSD_CHUNK, SSD_CHUNK), 1)
    return si <= qi if forward else si >= qi


def _ssd_prepare(bm, cm, dt_raw, bias, a_neg, forward, pre_ref):
    tri = jnp.where(_ssd_keep(forward), 1.0, 0.0).astype(BF16)
    dtc = _softplus(dt_raw + bias)
    a = dtc * (a_neg * LOG2E)
    hi, mid, lo = _split3(a)
    cum = (jnp.dot(tri, hi, preferred_element_type=F32)
           + jnp.dot(tri, mid, preferred_element_type=F32)
           + jnp.dot(tri, lo, preferred_element_type=F32))
    pre_ref[0] = cum
    pre_ref[1] = cum.T - jnp.log2(dtc.T)
    pre_ref[2] = lax.dot_general(cm.astype(BF16), bm.astype(BF16), _NT,
                                 preferred_element_type=F32)
    pre_ref[3] = bm.T


def _ssd_main(xs, cm, pre_ref, h_ref, y_ref, row0, lane0, forward, dskip):
    Q = SSD_CHUNK
    keep = _ssd_keep(forward)
    last = Q - 1 if forward else 0
    cum = pre_ref[0]
    row_t = pre_ref[1]
    gm = pre_ref[2]
    bt = pre_ref[3]
    lane = lax.broadcasted_iota(jnp.int32, (Q, LANES), 1)
    first_half = lane < SSD_HEAD_DIM

    for pair in range(SSD_HEADS_PER_GROUP // 2):
        cols = slice(pair * LANES, (pair + 1) * LANES)
        xs_pair = xs[:, cols]
        h_pair = h_ref[:, cols]
        y_pair = None
        s_pair = None
        decs = []
        for e in range(2):
            ln = lane0 + 2 * pair + e
            hmask = first_half if e == 0 else jnp.logical_not(first_half)
            colb = jnp.broadcast_to(cum[:, ln:ln + 1], (Q, Q))
            rowb = jnp.broadcast_to(row_t[ln:ln + 1, :], (Q, Q))
            decay = jnp.where(keep, jnp.exp2(colb - rowb), 0.0)
            m1 = (gm * decay).astype(BF16)
            m2 = (cm * jnp.exp2(colb)).astype(BF16)
            lhs = jnp.concatenate([m1, m2], axis=1)
            xr = jnp.where(hmask, xs_pair, 0.0).astype(BF16)
            hr = jnp.where(hmask, h_pair, 0.0).astype(BF16)
            rhs = jnp.concatenate([xr, hr], axis=0)
            yc = jnp.dot(lhs, rhs, preferred_element_type=F32)
            y_pair = yc if y_pair is None else y_pair + yc
            tot = colb[last:last + 1, :]
            wb = (bt * jnp.exp2(tot - rowb)).astype(BF16)
            sc = jnp.dot(wb, xr, preferred_element_type=F32)
            s_pair = sc if s_pair is None else s_pair + sc
            decs.append(jnp.exp2(tot))
        dec = jnp.where(first_half[0:1, :], decs[0], decs[1])
        h_ref[:, cols] = h_pair * dec + s_pair
        if dskip is not None:
            y_pair = y_pair + dskip[:, cols] * xs_pair
        y_ref[pl.ds(row0, Q), cols] = y_pair


def _ssd_kernel(xsf_ref, bf_ref, cf_ref, dtf_ref, xsb_ref, bb_ref, cb_ref, dtb_ref,
                bias_ref, alog_ref, dsk_ref, yf_ref, yb_ref, hf_ref, hb_ref, pre_ref, *, nck):
    @pl.when(pl.program_id(2) == 0)
    def _():
        hf_ref[...] = jnp.zeros_like(hf_ref)
        hb_ref[...] = jnp.zeros_like(hb_ref)

    bias = bias_ref[...]
    a_neg = -jnp.exp(alog_ref[...])
    dsk = dsk_ref[...]
    Q = SSD_CHUNK

    def rows(j):
        if isinstance(j, int):
            return j * Q, (nck - 1 - j) * Q
        return pl.multiple_of(j * Q, Q), pl.multiple_of((nck - 1 - j) * Q, Q)

    def prepare(j, slot):
        rf, rb = rows(j)
        _ssd_prepare(bf_ref[pl.ds(rf, Q), :], cf_ref[pl.ds(rf, Q), :], dtf_ref[pl.ds(rf, Q), :],
                     bias, a_neg, True, pre_ref.at[slot, 0])
        _ssd_prepare(bb_ref[pl.ds(rb, Q), :], cb_ref[pl.ds(rb, Q), :], dtb_ref[pl.ds(rb, Q), :],
                     bias, a_neg, False, pre_ref.at[slot, 1])

    def main(j, slot):
        rf, rb = rows(j)
        _ssd_main(xsf_ref[pl.ds(rf, Q), :], cf_ref[pl.ds(rf, Q), :], pre_ref.at[slot, 0],
                  hf_ref, yf_ref, rf, 0, True, dsk)
        _ssd_main(xsb_ref[pl.ds(rb, Q), :], cb_ref[pl.ds(rb, Q), :], pre_ref.at[slot, 1],
                  hb_ref, yb_ref, rb, SSD_HEADS_PER_GROUP, False, None)

    prepare(0, 0)

    def body(t, carry):
        for u in range(SSD_STEPS_PER_TRIP):
            j = SSD_STEPS_PER_TRIP * t + u
            prepare(jnp.minimum(j + 1, nck - 1), (u + 1) % 2)
            main(j, u % 2)
        return carry

    lax.fori_loop(0, nck // SSD_STEPS_PER_TRIP, body, 0)


def _ssd(xc, dt, bias, alog, dskip, batch, seq, rb):
    T = xc.shape[0]
    rb = min(rb, seq)
    nb = seq // rb
    nck = rb // SSD_CHUNK
    assert nck % SSD_STEPS_PER_TRIP == 0 and nb * rb == seq
    gw = SSD_HEADS_PER_GROUP * SSD_HEAD_DIM
    b_off = SSD_INNER // LANES
    c_off = b_off + SSD_GROUPS

    def fwd(col):
        return lambda b, g, i: (b * nb + i, col(g))

    def bwd(col):
        return lambda b, g, i: (b * nb + nb - 1 - i, col(g))

    def specs(mk):
        return [pl.BlockSpec((rb, gw), mk(lambda g: g)),
                pl.BlockSpec((rb, LANES), mk(lambda g: b_off + g)),
                pl.BlockSpec((rb, LANES), mk(lambda g: c_off + g)),
                pl.BlockSpec((rb, LANES), mk(lambda g: g))]

    out_sds = jax.ShapeDtypeStruct((T, SSD_INNER), F32)
    return pl.pallas_call(
        functools.partial(_ssd_kernel, nck=nck),
        grid=(batch, SSD_GROUPS, nb),
        in_specs=specs(fwd) + specs(bwd) + [
            pl.BlockSpec((None, 1, LANES), lambda b, g, i: (g, 0, 0)),
            pl.BlockSpec((None, 1, LANES), lambda b, g, i: (g, 0, 0)),
            pl.BlockSpec((1, gw), lambda b, g, i: (0, g))],
        out_specs=[pl.BlockSpec((rb, gw), fwd(lambda g: g)),
                   pl.BlockSpec((rb, gw), bwd(lambda g: g))],
        out_shape=[out_sds, out_sds],
        scratch_shapes=[pltpu.VMEM((SSD_STATE, gw), F32), pltpu.VMEM((SSD_STATE, gw), F32),
                        pltpu.VMEM((2, 2, 4, SSD_CHUNK, LANES), F32)],
        compiler_params=_cparams(("parallel", "parallel", "arbitrary")),
        name="ssd_scan",
    )(xc, xc, xc, dt, xc, xc, xc, dt, bias, alog, dskip)


def _attn_kernel(lam_ref, q_ref, k_ref, vt_ref, dg_ref, nw_ref, o_ref,
                 qm_ref, s_ref, bm_ref, m_ref, acc_ref, *, tk, nk, out_scale):
    q = q_ref[...]
    lane = lax.broadcasted_iota(jnp.int32, q.shape, 1)
    map0 = (lane % SSD_HEAD_DIM) < (DIFF_HEAD_DIM // 2)
    zero = jnp.zeros_like(q)
    qm_ref[0] = jnp.where(map0, q, zero)
    qm_ref[1] = jnp.where(map0, zero, q)
    m_ref[...] = jnp.full_like(m_ref, -jnp.inf)
    acc_ref[...] = jnp.zeros_like(acc_ref)
    ones = jnp.ones((ONES_ROWS, tk), BF16)
    dv = 2 * DIFF_HEAD_DIM

    def key_start(blk):
        return blk * tk if isinstance(blk, int) else pl.multiple_of(blk * tk, tk)

    def scores(blk, slot):
        k = k_ref[pl.ds(key_start(blk), tk), :]
        for c in range(2):
            s = lax.dot_general(k, qm_ref[c], _NT, preferred_element_type=F32)
            s_ref[slot, c] = s
            bm_ref[slot, c] = jnp.max(s, axis=0, keepdims=True)

    def softmax_pv(blk, slot):
        vt = vt_ref[:, pl.ds(key_start(blk), tk)]
        lhs = jnp.concatenate([vt, ones], axis=0)
        for c in range(2):
            s = s_ref[slot, c]
            m_old = m_ref[c]
            m_new = jnp.maximum(m_old, bm_ref[slot, c])
            alpha = jnp.exp2(m_old - m_new)
            p = jnp.exp2(s - m_new).astype(BF16)
            acc_ref[c] = acc_ref[c] * alpha + jnp.dot(lhs, p, preferred_element_type=F32)
            m_ref[c] = m_new

    scores(0, 0)

    def pair(p):
        scores(2 * p + 1, 1)
        softmax_pv(2 * p, 0)
        scores(2 * p + 2, 0)
        softmax_pv(2 * p + 1, 1)

    n_pairs = nk // 2 - 1
    trips = n_pairs // PAIRS_PER_TRIP

    def body(t, carry):
        for u in range(PAIRS_PER_TRIP):
            pair(PAIRS_PER_TRIP * t + u)
        return carry

    lax.fori_loop(0, trips, body, 0)
    for p in range(trips * PAIRS_PER_TRIP, n_pairs):
        pair(p)
    scores(nk - 1, 1)
    softmax_pv(nk - 2, 0)
    softmax_pv(nk - 1, 1)

    lam = lam_ref[0, 0]
    o = (acc_ref[0, :dv, :] * (1.0 / acc_ref[0, dv:dv + 1, :])
         - lam * (acc_ref[1, :dv, :] * (1.0 / acc_ref[1, dv:dv + 1, :])))
    ms = jnp.mean(o * o, axis=0, keepdims=True)
    o = o * lax.rsqrt(ms + NORM_EPS)
    ot = o.T * (nw_ref[...] * out_scale)
    o_ref[...] = (ot * _silu(dg_ref[...])).astype(o_ref.dtype)


def _diff_attention(lam, q, k, vt, dg, norm_w, batch, seq, out_scale, tq, tk):
    T = q.shape[0]
    tq, tk = min(tq, seq), min(tk, seq // 2)
    nq = seq // tq
    nk = seq // tk
    assert nk % 2 == 0 and nk * tk == seq
    return pl.pallas_call(
        functools.partial(_attn_kernel, tk=tk, nk=nk, out_scale=out_scale),
        grid=(batch, DIFF_HEADS, nq),
        in_specs=[pl.BlockSpec(memory_space=pltpu.SMEM),
                  pl.BlockSpec((tq, LANES), lambda b, h, i: (b * nq + i, h)),
                  pl.BlockSpec((seq, LANES), lambda b, h, i: (b, h)),
                  pl.BlockSpec((None, LANES, seq), lambda b, h, i: (b, h, 0)),
                  pl.BlockSpec((tq, LANES), lambda b, h, i: (b * nq + i, h)),
                  pl.BlockSpec((1, LANES), lambda b, h, i: (0, 0))],
        out_specs=pl.BlockSpec((tq, LANES), lambda b, h, i: (b * nq + i, h)),
        out_shape=jax.ShapeDtypeStruct((T, DIFF_WIDTH), BF16),
        scratch_shapes=[pltpu.VMEM((2, tq, LANES), BF16),
                        pltpu.VMEM((2, 2, tk, tq), F32),
                        pltpu.VMEM((2, 2, 1, tq), F32),
                        pltpu.VMEM((2, 1, tq), F32),
                        pltpu.VMEM((2, LANES + ONES_ROWS, tq), F32)],
        compiler_params=_cparams(("parallel", "parallel", "parallel")),
        name="diff_attention",
    )(lam, q, k, vt, dg, norm_w)


def _tail_kernel(x_ref, xb_ref, yf_ref, yb_ref, yd_ref, kv_ref,
                 wz_ref, wcq_ref, wcg_ref, wgl_ref, wbs_ref, wbd_ref, wbc_ref, wo_ref,
                 nw_ref, gb_ref, lng_ref, lnb_ref, xo_ref, xbo_ref):
    xb = xb_ref[...]

    def gate(idx):
        cols = slice(idx * D_MODEL, (idx + 1) * D_MODEL)
        gl = jnp.dot(xb, wgl_ref[:, cols], preferred_element_type=F32) + gb_ref[:, cols]
        return _sigmoid(gl)

    z = jnp.dot(xb, wz_ref[...], preferred_element_type=F32)
    y = (yf_ref[...] + yb_ref[...]) * _silu(z)
    y = y * lax.rsqrt(jnp.mean(y * y, axis=-1, keepdims=True) + NORM_EPS) * nw_ref[...]
    merged = gate(0) * jnp.dot(y.astype(BF16), wbs_ref[...], preferred_element_type=F32)

    merged = merged + gate(1) * jnp.dot(yd_ref[...], wbd_ref[...], preferred_element_type=F32)

    cq = (jnp.dot(xb, wcq_ref[...], preferred_element_type=F32)
          * (CROSS_HEAD_DIM ** -0.5)).astype(BF16)
    cg = jnp.dot(xb, wcg_ref[...], preferred_element_type=F32)
    outs = []
    for h in range(CROSS_HEADS):
        cols = slice(h * CROSS_HEAD_DIM, (h + 1) * CROSS_HEAD_DIM)
        mk = kv_ref[:, cols]
        mv = kv_ref[:, CROSS_WIDTH + h * CROSS_HEAD_DIM:CROSS_WIDTH + (h + 1) * CROSS_HEAD_DIM]
        s = lax.dot_general(cq[:, cols], mk, _NT, preferred_element_type=F32)
        e = jnp.exp(s - jnp.max(s, axis=-1, keepdims=True))
        p = e * (1.0 / jnp.sum(e, axis=-1, keepdims=True))
        outs.append(jnp.dot(p.astype(BF16), mv, preferred_element_type=F32))
    yc = jnp.concatenate(outs, axis=1) * _silu(cg)
    merged = merged + gate(2) * jnp.dot(yc.astype(BF16), wbc_ref[...],
                                        preferred_element_type=F32)

    out = jnp.dot(merged.astype(BF16), wo_ref[...], preferred_element_type=F32)
    r = DEEPNORM_ALPHA * x_ref[...] + out
    mu = jnp.mean(r, axis=-1, keepdims=True)
    d = r - mu
    var = jnp.mean(d * d, axis=-1, keepdims=True)
    xn = d * lax.rsqrt(var + NORM_EPS) * lng_ref[...] + lnb_ref[...]
    xo_ref[...] = xn
    xbo_ref[...] = xn.astype(BF16)


def _tail(x, xb, yf, yb, yd, kv, wz, wcq, wcg, wgl, wbs, wbd, wbc, wo, nw, gb, lng, lnb,
          seq, tm):
    T = x.shape[0]
    tm = min(tm, seq)
    tps = seq // tm
    mem_tokens = kv.shape[0] // (T // seq)
    row = lambda w: pl.BlockSpec((tm, w), lambda i: (i, 0))
    const = lambda a: pl.BlockSpec(a.shape, lambda i: (0, 0), pipeline_mode=pl.Buffered(1))
    return pl.pallas_call(
        _tail_kernel,
        grid=(T // tm,),
        in_specs=[row(D_MODEL), row(D_MODEL), row(SSD_INNER), row(SSD_INNER), row(DIFF_WIDTH),
                  pl.BlockSpec((mem_tokens, 2 * CROSS_WIDTH), lambda i: (i // tps, 0)),
                  const(wz), const(wcq), const(wcg), const(wgl), const(wbs), const(wbd),
                  const(wbc), const(wo), const(nw), const(gb), const(lng), const(lnb)],
        out_specs=[row(D_MODEL), row(D_MODEL)],
        out_shape=[jax.ShapeDtypeStruct((T, D_MODEL), F32),
                   jax.ShapeDtypeStruct((T, D_MODEL), BF16)],
        compiler_params=_cparams(("parallel",)),
        name="tail",
    )(x, xb, yf, yb, yd, kv, wz, wcq, wcg, wgl, wbs, wbd, wbc, wo, nw, gb, lng, lnb)


def _in_splits():
    sizes = [SSD_INNER, SSD_CONV_CH, 2 * SSD_HEADS, DIFF_WIDTH, DIFF_WIDTH, DIFF_WIDTH,
             DIFF_WIDTH, CROSS_WIDTH, CROSS_WIDTH, N_BRANCH * D_MODEL]
    offs = np.concatenate([[0], np.cumsum(sizes)])
    return [(int(offs[i]), int(offs[i + 1])) for i in range(len(sizes))]


def _rope_perm():
    half = DIFF_HEAD_DIM // 2
    perm = np.zeros(DIFF_WIDTH, np.int32)
    for h in range(DIFF_HEADS):
        for l in range(LANES):
            hf, c, j = l // 64, (l % 64) // half, l % half
            perm[h * LANES + l] = h * LANES + c * DIFF_HEAD_DIM + hf * half + j
    return perm


def _dt_placement():
    cols = np.zeros(2 * SSD_HEADS, np.int32)
    for d in range(2):
        for g in range(SSD_GROUPS):
            for r in range(SSD_HEADS_PER_GROUP):
                cols[d * SSD_HEADS + g * SSD_HEADS_PER_GROUP + r] = (
                    g * LANES + d * SSD_HEADS_PER_GROUP + r)
    return cols


def _per_group_lanes(v):
    out = jnp.zeros((SSD_GROUPS * LANES,), F32).at[_dt_placement()].set(v.reshape(-1).astype(F32))
    return out.reshape(SSD_GROUPS, 1, LANES)


def kernel(x, mem, positions, w_in, conv_w, conv_b, dt_bias, a_log, d_skip, ssd_norm_w,
           diff_lam, diff_norm_w, w_mem_kv, w_br_ssd, w_br_diff, w_br_cross, gate_b, w_out,
           ln_g, ln_b):
    batch, seq, _ = x.shape
    T = batch * seq
    mem_tokens = mem.shape[1]
    (sz, sxbc, sdt, sdq, sdk, sdv, sdg, scq, scg, sgl) = _in_splits()
    perm = _rope_perm()
    dt_cols = _dt_placement()

    inv = 1.0 / (ROPE_THETA ** (jnp.arange(0, DIFF_HEAD_DIM, 2, dtype=F32) / DIFF_HEAD_DIM))
    ang = positions.astype(F32)[..., None] * inv
    cos, sin = jnp.cos(ang), jnp.sin(ang)
    cos_t = jnp.concatenate([cos, cos, cos, cos], axis=-1).reshape(T, LANES)
    sin_t = jnp.concatenate([-sin, -sin, sin, sin], axis=-1).reshape(T, LANES)

    xf = x.reshape(T, D_MODEL)
    xb = xf.astype(BF16)
    memb = mem.reshape(batch * mem_tokens, D_MODEL).astype(BF16)

    for layer in range(DEPTH):
        lambda_init = 0.8 - 0.6 * math.exp(-0.3 * layer)
        w = w_in[layer]
        wb = lambda s: w[:, s[0]:s[1]].astype(BF16)
        w_q = wb(sdq)[:, perm]
        w_k = wb(sdk)[:, perm]
        w_vt = wb(sdv).T
        w_dt = jnp.zeros((D_MODEL, SSD_GROUPS * LANES), BF16).at[:, dt_cols].set(wb(sdt))
        conv_w8 = jnp.zeros((8, SSD_CONV_CH), F32).at[:SSD_CONV].set(conv_w[layer])

        xc = _proj_conv(xb, wb(sxbc), conv_w8, conv_b[layer][None, :], seq, F32, 512, 1024)
        q, k, vt, dg, dt = _proj_attn(xb, w_q, w_k, w_vt, wb(sdg), w_dt, cos_t, sin_t,
                                      DIFF_HEAD_DIM ** -0.5 * LOG2E, batch, seq, 512)
        kv = _proj_plain(memb, w_mem_kv[layer].astype(BF16), BF16, 1024, 1024)

        yf, yb = _ssd(xc, dt, _per_group_lanes(dt_bias[layer]), _per_group_lanes(a_log[layer]),
                      jnp.repeat(d_skip[layer].astype(F32), SSD_HEAD_DIM)[None, :],
                      batch, seq, 1024)

        lq = diff_lam[layer].astype(F32)
        lam = (jnp.exp(jnp.sum(lq[0] * lq[1])) - jnp.exp(jnp.sum(lq[2] * lq[3]))
               + lambda_init).reshape(1, 1)
        yd = _diff_attention(lam, q, k, vt, dg, diff_norm_w[layer].astype(F32)[None, :],
                             batch, seq, 1.0 - lambda_init, 512, 512)

        xf, xb = _tail(xf, xb, yf, yb, yd, kv,
                       wb(sz), wb(scq), wb(scg), wb(sgl),
                       w_br_ssd[layer].astype(BF16), w_br_diff[layer].astype(BF16),
                       w_br_cross[layer].astype(BF16), w_out[layer].astype(BF16),
                       ssd_norm_w[layer].astype(F32)[None, :],
                       gate_b[layer].astype(F32).reshape(1, N_BRANCH * D_MODEL),
                       ln_g[layer].astype(F32)[None, :], ln_b[layer].astype(F32)[None, :],
                       seq, 256)

    return xf.reshape(batch, seq, D_MODEL).astype(x.dtype)
```

```python
import functools
import math

import numpy as np
import jax
import jax.numpy as jnp
from jax import lax
from jax.experimental import pallas as pl
from jax.experimental.pallas import tpu as pltpu

F32 = jnp.float32
BF16 = jnp.bfloat16

D_MODEL = 1024
DEPTH = 2
SSD_INNER = 2048
SSD_HEAD_DIM = 64
SSD_HEADS = 32
SSD_GROUPS = 4
SSD_HEADS_PER_GROUP = SSD_HEADS // SSD_GROUPS
SSD_STATE = 128
SSD_CONV = 5
SSD_CHUNK = 128
SSD_CONV_CH = SSD_INNER + 2 * SSD_GROUPS * SSD_STATE
DIFF_HEAD_DIM = 64
DIFF_HEADS = 8
DIFF_WIDTH = 1024
ROPE_THETA = 10000.0
CROSS_HEADS = 4
CROSS_HEAD_DIM = 256
CROSS_WIDTH = 1024
N_BRANCH = 3
DEEPNORM_ALPHA = (2 * DEPTH) ** 0.25
NORM_EPS = 1e-5

LANES = 128
HALO = 16
ONES_ROWS = 16
PAIRS_PER_TRIP = 3
SSD_STEPS_PER_TRIP = 2
LOG2E = 1.4426950408889634
VMEM_LIMIT = 56 * 1024 * 1024

_NT = (((1,), (1,)), ((), ()))


def _sigmoid(v):
    return 1.0 / (1.0 + jnp.exp(-v))


def _silu(v):
    return v * _sigmoid(v)


def _cparams(sem):
    return pltpu.CompilerParams(dimension_semantics=sem, vmem_limit_bytes=VMEM_LIMIT)


def _proj_plain_kernel(x_ref, w_ref, o_ref):
    o_ref[...] = jnp.dot(x_ref[...], w_ref[...],
                         preferred_element_type=F32).astype(o_ref.dtype)


def _proj_plain(xb, w, out_dtype, tm, tn):
    T, K = xb.shape
    N = w.shape[1]
    tm, tn = min(tm, T), min(tn, N)
    return pl.pallas_call(
        _proj_plain_kernel,
        grid=(N // tn, T // tm),
        in_specs=[pl.BlockSpec((tm, K), lambda j, i: (i, 0)),
                  pl.BlockSpec((K, tn), lambda j, i: (0, j))],
        out_specs=pl.BlockSpec((tm, tn), lambda j, i: (i, j)),
        out_shape=jax.ShapeDtypeStruct((T, N), out_dtype),
        compiler_params=_cparams(("parallel", "parallel")),
        name="proj_plain",
    )(xb, w)


def _proj_attn_kernel(x_ref, wq_ref, wk_ref, wvt_ref, wdg_ref, wdt_ref, pos_ref, inv_ref, sgn_ref,
                      q_ref, k_ref, vt_ref, dg_ref, dt_ref, *, q_scale):
    x = x_ref[...]
    ang = pos_ref[...].astype(F32) * inv_ref[...]
    c = jnp.cos(ang)
    s = jnp.sin(ang) * sgn_ref[...]

    def rope(w_ref, o_ref, scale):
        acc = jnp.dot(x, w_ref[...], preferred_element_type=F32)
        for j in range(acc.shape[1] // LANES):
            t = acc[:, j * LANES:(j + 1) * LANES]
            o = t * c + pltpu.roll(t, LANES // 2, axis=1) * s
            if scale != 1.0:
                o = o * scale
            o_ref[:, j * LANES:(j + 1) * LANES] = o.astype(o_ref.dtype)

    rope(wq_ref, q_ref, q_scale)
    rope(wk_ref, k_ref, 1.0)
    vt_ref[...] = lax.dot_general(wvt_ref[...], x, _NT,
                                  preferred_element_type=F32).astype(vt_ref.dtype)
    dg_ref[...] = jnp.dot(x, wdg_ref[...], preferred_element_type=F32)
    dt_ref[...] = jnp.dot(x, wdt_ref[...], preferred_element_type=F32)


def _proj_attn(xb, w_q, w_k, w_vt, w_dg, w_dt, pos, inv_t, sgn_t, q_scale, batch, seq, tm):
    T, K = xb.shape
    tm = min(tm, seq)
    ns = seq // tm
    row = lambda n: pl.BlockSpec((tm, n), lambda i: (i, 0))
    const = lambda a: pl.BlockSpec(a.shape, lambda i: (0, 0), pipeline_mode=pl.Buffered(1))
    n_v, n_dt = w_vt.shape[0], w_dt.shape[1]
    return pl.pallas_call(
        functools.partial(_proj_attn_kernel, q_scale=q_scale),
        grid=(T // tm,),
        in_specs=[row(K), const(w_q), const(w_k), const(w_vt), const(w_dg), const(w_dt),
                  row(1), const(inv_t), const(sgn_t)],
        out_specs=[row(DIFF_WIDTH), row(DIFF_WIDTH),
                   pl.BlockSpec((None, n_v, tm), lambda i: (i // ns, 0, i % ns)),
                   row(DIFF_WIDTH), row(n_dt)],
        out_shape=[jax.ShapeDtypeStruct((T, DIFF_WIDTH), BF16),
                   jax.ShapeDtypeStruct((T, DIFF_WIDTH), BF16),
                   jax.ShapeDtypeStruct((batch, n_v, seq), BF16),
                   jax.ShapeDtypeStruct((T, DIFF_WIDTH), F32),
                   jax.ShapeDtypeStruct((T, n_dt), F32)],
        compiler_params=_cparams(("parallel",)),
        name="proj_attn",
    )(xb, w_q, w_k, w_vt, w_dg, w_dt, pos, inv_t, sgn_t)


def _proj_conv_kernel(xp_ref, x_ref, xn_ref, w_ref, cw_ref, cb_ref, o_ref, xcat_ref,
                      *, tm, tiles_per_seq):
    i = pl.program_id(1)
    pos = i % tiles_per_seq
    xp = xp_ref[...]
    xn = xn_ref[...]
    xcat_ref[0:HALO, :] = jnp.where(pos == 0, jnp.zeros_like(xp), xp)
    xcat_ref[HALO:HALO + tm, :] = x_ref[...]
    xcat_ref[HALO + tm:, :] = jnp.where(pos == tiles_per_seq - 1, jnp.zeros_like(xn), xn)
    acc = jnp.dot(xcat_ref[...], w_ref[...], preferred_element_type=F32)
    rows = acc.shape[0]
    pad = SSD_CONV // 2
    out = cb_ref[...]
    for j in range(SSD_CONV):
        shifted = acc if j == pad else pltpu.roll(acc, (pad - j) % rows, axis=0)
        out = out + cw_ref[j:j + 1, :] * shifted[HALO:HALO + tm, :]
    o_ref[...] = _silu(out).astype(o_ref.dtype)


def _proj_conv(xb, w, conv_w8, conv_b, seq, out_dtype, tm, tn):
    T, K = xb.shape
    N = w.shape[1]
    tm, tn = min(tm, seq), min(tn, N)
    tps = seq // tm
    hb = tm // HALO
    nhb = T // HALO
    return pl.pallas_call(
        functools.partial(_proj_conv_kernel, tm=tm, tiles_per_seq=tps),
        grid=(N // tn, T // tm),
        in_specs=[pl.BlockSpec((HALO, K), lambda j, i: (jnp.maximum(i * hb - 1, 0), 0)),
                  pl.BlockSpec((tm, K), lambda j, i: (i, 0)),
                  pl.BlockSpec((HALO, K), lambda j, i: (jnp.minimum((i + 1) * hb, nhb - 1), 0)),
                  pl.BlockSpec((K, tn), lambda j, i: (0, j)),
                  pl.BlockSpec((8, tn), lambda j, i: (0, j)),
                  pl.BlockSpec((1, tn), lambda j, i: (0, j))],
        out_specs=pl.BlockSpec((tm, tn), lambda j, i: (i, j)),
        out_shape=jax.ShapeDtypeStruct((T, N), out_dtype),
        scratch_shapes=[pltpu.VMEM((tm + 2 * HALO, K), BF16)],
        compiler_params=_cparams(("parallel", "parallel")),
        name="proj_conv",
    )(xb, xb, xb, w, conv_w8, conv_b)


def _softplus(v):
    return jnp.maximum(v, 0.0) + jnp.log(1.0 + jnp.exp(-jnp.abs(v)))


def _split3(v):
    hi = v.astype(BF16)
    r1 = v - hi.astype(F32)
    mid = r1.astype(BF16)
    lo = (r1 - mid.astype(F32)).astype(BF16)
    return hi, mid, lo


def _ssd_keep(forward):
    qi = lax.broadcasted_iota(jnp.int32, (SSD_CHUNK, SSD_CHUNK), 0)
    si = lax.broadcasted_iota(jnp.int32, (SSD_CHUNK, SSD_CHUNK), 1)
    return si <= qi if forward else si >= qi


def _ssd_prepare(bm, cm, dt_raw, bias, a_neg, forward, pre_ref):
    tri = jnp.where(_ssd_keep(forward), 1.0, 0.0).astype(BF16)
    dtc = _softplus(dt_raw + bias)
    a = dtc * (a_neg * LOG2E)
    hi, mid, lo = _split3(a)
    cum = (jnp.dot(tri, hi, preferred_element_type=F32)
           + jnp.dot(tri, mid, preferred_element_type=F32)
           + jnp.dot(tri, lo, preferred_element_type=F32))
    pre_ref[0] = cum
    pre_ref[1] = cum.T - jnp.log2(dtc.T)
    pre_ref[2] = lax.dot_general(cm.astype(BF16), bm.astype(BF16), _NT,
                                 preferred_element_type=F32)
    pre_ref[3] = bm.T


def _ssd_main(xs, cm, pre_ref, h_ref, y_ref, row0, lane0, forward, dskip):
    Q = SSD_CHUNK
    keep = _ssd_keep(forward)
    last = Q - 1 if forward else 0
    cum = pre_ref[0]
    row_t = pre_ref[1]
    gm = pre_ref[2]
    bt = pre_ref[3]
    lane = lax.broadcasted_iota(jnp.int32, (Q, LANES), 1)
    first_half = lane < SSD_HEAD_DIM

    for pair in range(SSD_HEADS_PER_GROUP // 2):
        cols = slice(pair * LANES, (pair + 1) * LANES)
        xs_pair = xs[:, cols]
        h_pair = h_ref[:, cols]
        y_pair = None
        s_pair = None
        decs = []
        for e in range(2):
            ln = lane0 + 2 * pair + e
            hmask = first_half if e == 0 else jnp.logical_not(first_half)
            colb = jnp.broadcast_to(cum[:, ln:ln + 1], (Q, Q))
            rowb = jnp.broadcast_to(row_t[ln:ln + 1, :], (Q, Q))
            decay = jnp.where(keep, jnp.exp2(colb - rowb), 0.0)
            m1 = (gm * decay).astype(BF16)
            m2 = (cm * jnp.exp2(colb)).astype(BF16)
            lhs = jnp.concatenate([m1, m2], axis=1)
            xr = jnp.where(hmask, xs_pair, 0.0).astype(BF16)
            hr = jnp.where(hmask, h_pair, 0.0).astype(BF16)
            rhs = jnp.concatenate([xr, hr], axis=0)
            yc = jnp.dot(lhs, rhs, preferred_element_type=F32)
            y_pair = yc if y_pair is None else y_pair + yc
            tot = colb[last:last + 1, :]
            wb = (bt * jnp.exp2(tot - rowb)).astype(BF16)
            sc = jnp.dot(wb, xr, preferred_element_type=F32)
            s_pair = sc if s_pair is None else s_pair + sc
            decs.append(jnp.exp2(tot))
        dec = jnp.where(first_half[0:1, :], decs[0], decs[1])
        h_ref[:, cols] = h_pair * dec + s_pair
        if dskip is not None:
            y_pair = y_pair + dskip[:, cols] * xs_pair
        y_ref[pl.ds(row0, Q), cols] = y_pair


def _ssd_kernel(xsf_ref, bf_ref, cf_ref, dtf_ref, xsb_ref, bb_ref, cb_ref, dtb_ref,
                bias_ref, alog_ref, dsk_ref, yf_ref, yb_ref, hf_ref, hb_ref, pre_ref, *, nck):
    @pl.when(pl.program_id(2) == 0)
    def _():
        hf_ref[...] = jnp.zeros_like(hf_ref)
        hb_ref[...] = jnp.zeros_like(hb_ref)

    bias = bias_ref[...]
    a_neg = -jnp.exp(alog_ref[...])
    dsk = dsk_ref[...]
    Q = SSD_CHUNK

    def rows(j):
        if isinstance(j, int):
            return j * Q, (nck - 1 - j) * Q
        return pl.multiple_of(j * Q, Q), pl.multiple_of((nck - 1 - j) * Q, Q)

    def prepare(j, slot):
        rf, rb = rows(j)
        _ssd_prepare(bf_ref[pl.ds(rf, Q), :], cf_ref[pl.ds(rf, Q), :], dtf_ref[pl.ds(rf, Q), :],
                     bias, a_neg, True, pre_ref.at[slot, 0])
        _ssd_prepare(bb_ref[pl.ds(rb, Q), :], cb_ref[pl.ds(rb, Q), :], dtb_ref[pl.ds(rb, Q), :],
                     bias, a_neg, False, pre_ref.at[slot, 1])

    def main(j, slot):
        rf, rb = rows(j)
        _ssd_main(xsf_ref[pl.ds(rf, Q), :], cf_ref[pl.ds(rf, Q), :], pre_ref.at[slot, 0],
                  hf_ref, yf_ref, rf, 0, True, dsk)
        _ssd_main(xsb_ref[pl.ds(rb, Q), :], cb_ref[pl.ds(rb, Q), :], pre_ref.at[slot, 1],
                  hb_ref, yb_ref, rb, SSD_HEADS_PER_GROUP, False, None)

    prepare(0, 0)

    def body(t, carry):
        for u in range(SSD_STEPS_PER_TRIP):
            j = SSD_STEPS_PER_TRIP * t + u
            prepare(jnp.minimum(j + 1, nck - 1), (u + 1) % 2)
            main(j, u % 2)
        return carry

    lax.fori_loop(0, nck // SSD_STEPS_PER_TRIP, body, 0)


def _ssd(xc, dt, bias, alog, dskip, batch, seq, rb):
    T = xc.shape[0]
    rb = min(rb, seq)
    nb = seq // rb
    nck = rb // SSD_CHUNK
    assert nck % SSD_STEPS_PER_TRIP == 0 and nb * rb == seq
    gw = SSD_HEADS_PER_GROUP * SSD_HEAD_DIM
    b_off = SSD_INNER // LANES
    c_off = b_off + SSD_GROUPS

    def fwd(col):
        return lambda b, g, i: (b * nb + i, col(g))

    def bwd(col):
        return lambda b, g, i: (b * nb + nb - 1 - i, col(g))

    def specs(mk):
        return [pl.BlockSpec((rb, gw), mk(lambda g: g)),
                pl.BlockSpec((rb, LANES), mk(lambda g: b_off + g)),
                pl.BlockSpec((rb, LANES), mk(lambda g: c_off + g)),
                pl.BlockSpec((rb, LANES), mk(lambda g: g))]

    out_sds = jax.ShapeDtypeStruct((T, SSD_INNER), F32)
    return pl.pallas_call(
        functools.partial(_ssd_kernel, nck=nck),
        grid=(batch, SSD_GROUPS, nb),
        in_specs=specs(fwd) + specs(bwd) + [
            pl.BlockSpec((None, 1, LANES), lambda b, g, i: (g, 0, 0)),
            pl.BlockSpec((None, 1, LANES), lambda b, g, i: (g, 0, 0)),
            pl.BlockSpec((1, gw), lambda b, g, i: (0, g))],
        out_specs=[pl.BlockSpec((rb, gw), fwd(lambda g: g)),
                   pl.BlockSpec((rb, gw), bwd(lambda g: g))],
        out_shape=[out_sds, out_sds],
        scratch_shapes=[pltpu.VMEM((SSD_STATE, gw), F32), pltpu.VMEM((SSD_STATE, gw), F32),
                        pltpu.VMEM((2, 2, 4, SSD_CHUNK, LANES), F32)],
        compiler_params=_cparams(("parallel", "parallel", "arbitrary")),
        name="ssd_scan",
    )(xc, xc, xc, dt, xc, xc, xc, dt, bias, alog, dskip)


def _attn_kernel(lam_ref, q_ref, k_ref, vt_ref, dg_ref, nw_ref, o_ref,
                 qm_ref, s_ref, bm_ref, m_ref, acc_ref, *, tk, nk, out_scale):
    q = q_ref[...]
    lane = lax.broadcasted_iota(jnp.int32, q.shape, 1)
    map0 = (lane % SSD_HEAD_DIM) < (DIFF_HEAD_DIM // 2)
    zero = jnp.zeros_like(q)
    qm_ref[0] = jnp.where(map0, q, zero)
    qm_ref[1] = jnp.where(map0, zero, q)
    m_ref[...] = jnp.full_like(m_ref, -jnp.inf)
    acc_ref[...] = jnp.zeros_like(acc_ref)
    ones = jnp.ones((ONES_ROWS, tk), BF16)
    dv = 2 * DIFF_HEAD_DIM

    def key_start(blk):
        return blk * tk if isinstance(blk, int) else pl.multiple_of(blk * tk, tk)

    def scores(blk, slot):
        k = k_ref[pl.ds(key_start(blk), tk), :]
        for c in range(2):
            s = lax.dot_general(k, qm_ref[c], _NT, preferred_element_type=F32)
            s_ref[slot, c] = s
            bm_ref[slot, c] = jnp.max(s, axis=0, keepdims=True)

    def softmax_pv(blk, slot):
        vt = vt_ref[:, pl.ds(key_start(blk), tk)]
        lhs = jnp.concatenate([vt, ones], axis=0)
        for c in range(2):
            s = s_ref[slot, c]
            m_old = m_ref[c]
            m_new = jnp.maximum(m_old, bm_ref[slot, c])
            alpha = jnp.exp2(m_old - m_new)
            p = jnp.exp2(s - m_new).astype(BF16)
            acc_ref[c] = acc_ref[c] * alpha + jnp.dot(lhs, p, preferred_element_type=F32)
            m_ref[c] = m_new

    scores(0, 0)

    def pair(p):
        scores(2 * p + 1, 1)
        softmax_pv(2 * p, 0)
        scores(2 * p + 2, 0)
        softmax_pv(2 * p + 1, 1)

    n_pairs = nk // 2 - 1
    trips = n_pairs // PAIRS_PER_TRIP

    def body(t, carry):
        for u in range(PAIRS_PER_TRIP):
            pair(PAIRS_PER_TRIP * t + u)
        return carry

    lax.fori_loop(0, trips, body, 0)
    for p in range(trips * PAIRS_PER_TRIP, n_pairs):
        pair(p)
    scores(nk - 1, 1)
    softmax_pv(nk - 2, 0)
    softmax_pv(nk - 1, 1)

    lam = lam_ref[0, 0]
    o = (acc_ref[0, :dv, :] * (1.0 / acc_ref[0, dv:dv + 1, :])
         - lam * (acc_ref[1, :dv, :] * (1.0 / acc_ref[1, dv:dv + 1, :])))
    ms = jnp.mean(o * o, axis=0, keepdims=True)
    o = o * lax.rsqrt(ms + NORM_EPS)
    ot = o.T * (nw_ref[...] * out_scale)
    o_ref[...] = (ot * _silu(dg_ref[...])).astype(o_ref.dtype)


def _diff_attention(lam, q, k, vt, dg, norm_w, batch, seq, out_scale, tq, tk):
    T = q.shape[0]
    tq, tk = min(tq, seq), min(tk, seq // 2)
    nq = seq // tq
    nk = seq // tk
    assert nk % 2 == 0 and nk * tk == seq
    return pl.pallas_call(
        functools.partial(_attn_kernel, tk=tk, nk=nk, out_scale=out_scale),
        grid=(batch, DIFF_HEADS, nq),
        in_specs=[pl.BlockSpec(memory_space=pltpu.SMEM),
                  pl.BlockSpec((tq, LANES), lambda b, h, i: (b * nq + i, h)),
                  pl.BlockSpec((seq, LANES), lambda b, h, i: (b, h)),
                  pl.BlockSpec((None, LANES, seq), lambda b, h, i: (b, h, 0)),
                  pl.BlockSpec((tq, LANES), lambda b, h, i: (b * nq + i, h)),
                  pl.BlockSpec((1, LANES), lambda b, h, i: (0, 0))],
        out_specs=pl.BlockSpec((tq, LANES), lambda b, h, i: (b * nq + i, h)),
        out_shape=jax.ShapeDtypeStruct((T, DIFF_WIDTH), BF16),
        scratch_shapes=[pltpu.VMEM((2, tq, LANES), BF16),
                        pltpu.VMEM((2, 2, tk, tq), F32),
                        pltpu.VMEM((2, 2, 1, tq), F32),
                        pltpu.VMEM((2, 1, tq), F32),
                        pltpu.VMEM((2, LANES + ONES_ROWS, tq), F32)],
        compiler_params=_cparams(("parallel", "parallel", "parallel")),
        name="diff_attention",
    )(lam, q, k, vt, dg, norm_w)


def _tail_kernel(x_ref, xb_ref, yf_ref, yb_ref, yd_ref, kv_ref,
                 wz_ref, wcq_ref, wcg_ref, wgl_ref, wbs_ref, wbd_ref, wbc_ref, wo_ref,
                 nw_ref, gb_ref, lng_ref, lnb_ref, xo_ref, xbo_ref):
    xb = xb_ref[...]

    def gate(idx):
        cols = slice(idx * D_MODEL, (idx + 1) * D_MODEL)
        gl = jnp.dot(xb, wgl_ref[:, cols], preferred_element_type=F32) + gb_ref[:, cols]
        return _sigmoid(gl)

    z = jnp.dot(xb, wz_ref[...], preferred_element_type=F32)
    y = (yf_ref[...] + yb_ref[...]) * _silu(z)
    y = y * lax.rsqrt(jnp.mean(y * y, axis=-1, keepdims=True) + NORM_EPS) * nw_ref[...]
    merged = gate(0) * jnp.dot(y.astype(BF16), wbs_ref[...], preferred_element_type=F32)

    merged = merged + gate(1) * jnp.dot(yd_ref[...], wbd_ref[...], preferred_element_type=F32)

    cq = (jnp.dot(xb, wcq_ref[...], preferred_element_type=F32)
          * (CROSS_HEAD_DIM ** -0.5)).astype(BF16)
    cg = jnp.dot(xb, wcg_ref[...], preferred_element_type=F32)
    outs = []
    for h in range(CROSS_HEADS):
        cols = slice(h * CROSS_HEAD_DIM, (h + 1) * CROSS_HEAD_DIM)
        mk = kv_ref[:, cols]
        mv = kv_ref[:, CROSS_WIDTH + h * CROSS_HEAD_DIM:CROSS_WIDTH + (h + 1) * CROSS_HEAD_DIM]
        s = lax.dot_general(cq[:, cols], mk, _NT, preferred_element_type=F32)
        e = jnp.exp(s - jnp.max(s, axis=-1, keepdims=True))
        p = e * (1.0 / jnp.sum(e, axis=-1, keepdims=True))
        outs.append(jnp.dot(p.astype(BF16), mv, preferred_element_type=F32))
    yc = jnp.concatenate(outs, axis=1) * _silu(cg)
    merged = merged + gate(2) * jnp.dot(yc.astype(BF16), wbc_ref[...],
                                        preferred_element_type=F32)

    out = jnp.dot(merged.astype(BF16), wo_ref[...], preferred_element_type=F32)
    r = DEEPNORM_ALPHA * x_ref[...] + out
    mu = jnp.mean(r, axis=-1, keepdims=True)
    d = r - mu
    var = jnp.mean(d * d, axis=-1, keepdims=True)
    xn = d * lax.rsqrt(var + NORM_EPS) * lng_ref[...] + lnb_ref[...]
    xo_ref[...] = xn
    xbo_ref[...] = xn.astype(BF16)


def _tail(x, xb, yf, yb, yd, kv, wz, wcq, wcg, wgl, wbs, wbd, wbc, wo, nw, gb, lng, lnb,
          seq, tm):
    T = x.shape[0]
    tm = min(tm, seq)
    tps = seq // tm
    mem_tokens = kv.shape[0] // (T // seq)
    row = lambda w: pl.BlockSpec((tm, w), lambda i: (i, 0))
    const = lambda a: pl.BlockSpec(a.shape, lambda i: (0, 0), pipeline_mode=pl.Buffered(1))
    return pl.pallas_call(
        _tail_kernel,
        grid=(T // tm,),
        in_specs=[row(D_MODEL), row(D_MODEL), row(SSD_INNER), row(SSD_INNER), row(DIFF_WIDTH),
                  pl.BlockSpec((mem_tokens, 2 * CROSS_WIDTH), lambda i: (i // tps, 0)),
                  const(wz), const(wcq), const(wcg), const(wgl), const(wbs), const(wbd),
                  const(wbc), const(wo), const(nw), const(gb), const(lng), const(lnb)],
        out_specs=[row(D_MODEL), row(D_MODEL)],
        out_shape=[jax.ShapeDtypeStruct((T, D_MODEL), F32),
                   jax.ShapeDtypeStruct((T, D_MODEL), BF16)],
        compiler_params=_cparams(("parallel",)),
        name="tail",
    )(x, xb, yf, yb, yd, kv, wz, wcq, wcg, wgl, wbs, wbd, wbc, wo, nw, gb, lng, lnb)


def _in_splits():
    sizes = [SSD_INNER, SSD_CONV_CH, 2 * SSD_HEADS, DIFF_WIDTH, DIFF_WIDTH, DIFF_WIDTH,
             DIFF_WIDTH, CROSS_WIDTH, CROSS_WIDTH, N_BRANCH * D_MODEL]
    offs = np.concatenate([[0], np.cumsum(sizes)])
    return [(int(offs[i]), int(offs[i + 1])) for i in range(len(sizes))]


def _rope_perm():
    half = DIFF_HEAD_DIM // 2
    perm = np.zeros(DIFF_WIDTH, np.int32)
    for h in range(DIFF_HEADS):
        for l in range(LANES):
            hf, c, j = l // 64, (l % 64) // half, l % half
            perm[h * LANES + l] = h * LANES + c * DIFF_HEAD_DIM + hf * half + j
    return perm


def _dt_placement():
    cols = np.zeros(2 * SSD_HEADS, np.int32)
    for d in range(2):
        for g in range(SSD_GROUPS):
            for r in range(SSD_HEADS_PER_GROUP):
                cols[d * SSD_HEADS + g * SSD_HEADS_PER_GROUP + r] = (
                    g * LANES + d * SSD_HEADS_PER_GROUP + r)
    return cols


def _per_group_lanes(v):
    out = jnp.zeros((SSD_GROUPS * LANES,), F32).at[_dt_placement()].set(v.reshape(-1).astype(F32))
    return out.reshape(SSD_GROUPS, 1, LANES)


def kernel(x, mem, positions, w_in, conv_w, conv_b, dt_bias, a_log, d_skip, ssd_norm_w,
           diff_lam, diff_norm_w, w_mem_kv, w_br_ssd, w_br_diff, w_br_cross, gate_b, w_out,
           ln_g, ln_b):
    batch, seq, _ = x.shape
    T = batch * seq
    mem_tokens = mem.shape[1]
    (sz, sxbc, sdt, sdq, sdk, sdv, sdg, scq, scg, sgl) = _in_splits()
    perm = _rope_perm()
    dt_cols = _dt_placement()

    inv = 1.0 / (ROPE_THETA ** (jnp.arange(0, DIFF_HEAD_DIM, 2, dtype=F32) / DIFF_HEAD_DIM))
    inv_t = jnp.tile(inv, LANES // inv.shape[0])[None, :]
    sgn_t = jnp.where(jnp.arange(LANES) < LANES // 2, -1.0, 1.0).astype(F32)[None, :]
    pos = positions.reshape(T, 1)

    xf = x.reshape(T, D_MODEL)
    xb = xf.astype(BF16)
    memb = mem.reshape(batch * mem_tokens, D_MODEL).astype(BF16)

    for layer in range(DEPTH):
        lambda_init = 0.8 - 0.6 * math.exp(-0.3 * layer)
        w = w_in[layer]
        wb = lambda s: w[:, s[0]:s[1]].astype(BF16)
        w_q = wb(sdq)[:, perm]
        w_k = wb(sdk)[:, perm]
        w_vt = wb(sdv).T
        w_dt = jnp.zeros((D_MODEL, SSD_GROUPS * LANES), BF16).at[:, dt_cols].set(wb(sdt))
        conv_w8 = jnp.zeros((8, SSD_CONV_CH), F32).at[:SSD_CONV].set(conv_w[layer])

        xc = _proj_conv(xb, wb(sxbc), conv_w8, conv_b[layer][None, :], seq, F32, 512, 1024)
        q, k, vt, dg, dt = _proj_attn(xb, w_q, w_k, w_vt, wb(sdg), w_dt, pos, inv_t, sgn_t,
                                      DIFF_HEAD_DIM ** -0.5 * LOG2E, batch, seq, 512)
        kv = _proj_plain(memb, w_mem_kv[layer].astype(BF16), BF16, 1024, 1024)

        yf, yb = _ssd(xc, dt, _per_group_lanes(dt_bias[layer]), _per_group_lanes(a_log[layer]),
                      jnp.repeat(d_skip[layer].astype(F32), SSD_HEAD_DIM)[None, :],
                      batch, seq, 2048)

        lq = diff_lam[layer].astype(F32)
        lam = (jnp.exp(jnp.sum(lq[0] * lq[1])) - jnp.exp(jnp.sum(lq[2] * lq[3]))
               + lambda_init).reshape(1, 1)
        yd = _diff_attention(lam, q, k, vt, dg, diff_norm_w[layer].astype(F32)[None, :],
                             batch, seq, 1.0 - lambda_init, 512, 512)

        xf, xb = _tail(xf, xb, yf, yb, yd, kv,
                       wb(sz), wb(scq), wb(scg), wb(sgl),
                       w_br_ssd[layer].astype(BF16), w_br_diff[layer].astype(BF16),
                       w_br_cross[layer].astype(BF16), w_out[layer].astype(BF16),
                       ssd_norm_w[layer].astype(F32)[None, :],
                       gate_b[layer].astype(F32).reshape(1, N_BRANCH * D_MODEL),
                       ln_g[layer].astype(F32)[None, :], ln_b[layer].astype(F32)[None, :],
                       seq, 256)

    return xf.reshape(batch, seq, D_MODEL).astype(x.dtype)
```

```python
import functools
import math

import numpy as np
import jax
import jax.numpy as jnp
from jax import lax
from jax.experimental import pallas as pl
from jax.experimental.pallas import tpu as pltpu

F32 = jnp.float32
BF16 = jnp.bfloat16

D_MODEL = 1024
DEPTH = 2
SSD_INNER = 2048
SSD_HEAD_DIM = 64
SSD_HEADS = 32
SSD_GROUPS = 4
SSD_HEADS_PER_GROUP = SSD_HEADS // SSD_GROUPS
SSD_STATE = 128
SSD_CONV = 5
SSD_CHUNK = 128
SSD_CONV_CH = SSD_INNER + 2 * SSD_GROUPS * SSD_STATE
DIFF_HEAD_DIM = 64
DIFF_HEADS = 8
DIFF_WIDTH = 1024
ROPE_THETA = 10000.0
CROSS_HEADS = 4
CROSS_HEAD_DIM = 256
CROSS_WIDTH = 1024
N_BRANCH = 3
DEEPNORM_ALPHA = (2 * DEPTH) ** 0.25
NORM_EPS = 1e-5

LANES = 128
HALO = 16
ROW_PHASES = 4
ONES_ROWS = 16
PAIRS_PER_TRIP = 3
SSD_STEPS_PER_TRIP = 2
LOG2E = 1.4426950408889634
VMEM_LIMIT = 56 * 1024 * 1024

_NT = (((1,), (1,)), ((), ()))


def _sigmoid(v):
    return 1.0 / (1.0 + jnp.exp(-v))


def _silu(v):
    return v * _sigmoid(v)


def _cparams(sem):
    return pltpu.CompilerParams(dimension_semantics=sem, vmem_limit_bytes=VMEM_LIMIT)


def _proj_plain_kernel(x_ref, w_ref, o_ref):
    o_ref[...] = jnp.dot(x_ref[...], w_ref[...],
                         preferred_element_type=F32).astype(o_ref.dtype)


def _proj_plain(xb, w, out_dtype, tm, tn):
    T, K = xb.shape
    N = w.shape[1]
    tm, tn = min(tm, T), min(tn, N)
    return pl.pallas_call(
        _proj_plain_kernel,
        grid=(N // tn, T // tm),
        in_specs=[pl.BlockSpec((tm, K), lambda j, i: (i, 0)),
                  pl.BlockSpec((K, tn), lambda j, i: (0, j))],
        out_specs=pl.BlockSpec((tm, tn), lambda j, i: (i, j)),
        out_shape=jax.ShapeDtypeStruct((T, N), out_dtype),
        compiler_params=_cparams(("parallel", "parallel")),
        name="proj_plain",
    )(xb, w)


def _proj_attn_kernel(x_ref, wq_ref, wk_ref, wvt_ref, wdg_ref, wdt_ref, pos_ref, inv_ref, sgn_ref,
                      q_ref, k_ref, vt_ref, dg_ref, dt_ref, *, q_scale):
    x = x_ref[...]
    ang = pos_ref[...].astype(F32) * inv_ref[...]
    c = jnp.cos(ang)
    s = jnp.sin(ang) * sgn_ref[...]

    def rope(w_ref, o_ref, scale):
        acc = jnp.dot(x, w_ref[...], preferred_element_type=F32)
        for j in range(acc.shape[1] // LANES):
            t = acc[:, j * LANES:(j + 1) * LANES]
            o = t * c + pltpu.roll(t, LANES // 2, axis=1) * s
            if scale != 1.0:
                o = o * scale
            o_ref[:, j * LANES:(j + 1) * LANES] = o.astype(o_ref.dtype)

    rope(wq_ref, q_ref, q_scale)
    rope(wk_ref, k_ref, 1.0)
    vt_ref[...] = lax.dot_general(wvt_ref[...], x, _NT,
                                  preferred_element_type=F32).astype(vt_ref.dtype)
    dg_ref[...] = jnp.dot(x, wdg_ref[...], preferred_element_type=F32)
    dt_ref[...] = jnp.dot(x, wdt_ref[...], preferred_element_type=F32)


def _proj_attn(xb, w_q, w_k, w_vt, w_dg, w_dt, pos, inv_t, sgn_t, q_scale, batch, seq, tm):
    T, K = xb.shape
    tm = min(tm, seq)
    ns = seq // tm
    row = lambda n: pl.BlockSpec((tm, n), lambda i: (i, 0))
    const = lambda a: pl.BlockSpec(a.shape, lambda i: (0, 0), pipeline_mode=pl.Buffered(1))
    n_v, n_dt = w_vt.shape[0], w_dt.shape[1]
    return pl.pallas_call(
        functools.partial(_proj_attn_kernel, q_scale=q_scale),
        grid=(T // tm,),
        in_specs=[row(K), const(w_q), const(w_k), const(w_vt), const(w_dg), const(w_dt),
                  row(1), const(inv_t), const(sgn_t)],
        out_specs=[row(DIFF_WIDTH), row(DIFF_WIDTH),
                   pl.BlockSpec((None, n_v, tm), lambda i: (i // ns, 0, i % ns)),
                   row(DIFF_WIDTH), row(n_dt)],
        out_shape=[jax.ShapeDtypeStruct((T, DIFF_WIDTH), BF16),
                   jax.ShapeDtypeStruct((T, DIFF_WIDTH), BF16),
                   jax.ShapeDtypeStruct((batch, n_v, seq), BF16),
                   jax.ShapeDtypeStruct((T, DIFF_WIDTH), F32),
                   jax.ShapeDtypeStruct((T, n_dt), F32)],
        compiler_params=_cparams(("parallel",)),
        name="proj_attn",
    )(xb, w_q, w_k, w_vt, w_dg, w_dt, pos, inv_t, sgn_t)


def _proj_conv_kernel(xp_ref, x_ref, xn_ref, w_ref, cw_ref, cb_ref, o_ref, xcat_ref, acc_ref,
                      out_ref, *, tm, tiles_per_seq):
    i = pl.program_id(1)
    pos = i % tiles_per_seq
    xp = xp_ref[...]
    xn = xn_ref[...]
    xcat_ref[0:HALO, :] = jnp.where(pos == 0, jnp.zeros_like(xp), xp)
    xcat_ref[HALO:HALO + tm, :] = x_ref[...]
    xcat_ref[HALO + tm:, :] = jnp.where(pos == tiles_per_seq - 1, jnp.zeros_like(xn), xn)
    acc = jnp.dot(xcat_ref[...], w_ref[...], preferred_element_type=F32)
    n_slabs = acc.shape[1] // LANES
    for s in range(n_slabs):
        acc_ref[s] = acc[:, s * LANES:(s + 1) * LANES]
    pad = SSD_CONV // 2
    per_phase = tm // ROW_PHASES
    for s in range(n_slabs):
        lanes = slice(s * LANES, (s + 1) * LANES)
        for r in range(ROW_PHASES):
            out = cb_ref[:, lanes]
            for j in range(SSD_CONV):
                tap = acc_ref[s, pl.ds(HALO + r + j - pad, per_phase, stride=ROW_PHASES), :]
                out = out + cw_ref[j:j + 1, lanes] * tap
            out_ref[s, pl.ds(r, per_phase, stride=ROW_PHASES), :] = _silu(out)
        o_ref[:, lanes] = out_ref[s].astype(o_ref.dtype)


def _proj_conv(xb, w, conv_w8, conv_b, seq, out_dtype, tm, tn):
    T, K = xb.shape
    N = w.shape[1]
    tm, tn = min(tm, seq), min(tn, N)
    tps = seq // tm
    hb = tm // HALO
    nhb = T // HALO
    return pl.pallas_call(
        functools.partial(_proj_conv_kernel, tm=tm, tiles_per_seq=tps),
        grid=(N // tn, T // tm),
        in_specs=[pl.BlockSpec((HALO, K), lambda j, i: (jnp.maximum(i * hb - 1, 0), 0)),
                  pl.BlockSpec((tm, K), lambda j, i: (i, 0)),
                  pl.BlockSpec((HALO, K), lambda j, i: (jnp.minimum((i + 1) * hb, nhb - 1), 0)),
                  pl.BlockSpec((K, tn), lambda j, i: (0, j)),
                  pl.BlockSpec((8, tn), lambda j, i: (0, j)),
                  pl.BlockSpec((1, tn), lambda j, i: (0, j))],
        out_specs=pl.BlockSpec((tm, tn), lambda j, i: (i, j)),
        out_shape=jax.ShapeDtypeStruct((T, N), out_dtype),
        scratch_shapes=[pltpu.VMEM((tm + 2 * HALO, K), BF16),
                        pltpu.VMEM((tn // LANES, tm + 2 * HALO, LANES), F32),
                        pltpu.VMEM((tn // LANES, tm, LANES), F32)],
        compiler_params=_cparams(("parallel", "parallel")),
        name="proj_conv",
    )(xb, xb, xb, w, conv_w8, conv_b)


def _softplus(v):
    return jnp.maximum(v, 0.0) + jnp.log(1.0 + jnp.exp(-jnp.abs(v)))


def _split3(v):
    hi = v.astype(BF16)
    r1 = v - hi.astype(F32)
    mid = r1.astype(BF16)
    lo = (r1 - mid.astype(F32)).astype(BF16)
    return hi, mid, lo


def _ssd_keep(forward):
    qi = lax.broadcasted_iota(jnp.int32, (SSD_CHUNK, SSD_CHUNK), 0)
    si = lax.broadcasted_iota(jnp.int32, (SSD_CHUNK, SSD_CHUNK), 1)
    return si <= qi if forward else si >= qi


def _ssd_prepare(bm, cm, dt_raw, bias, a_neg, forward, pre_ref):
    tri = jnp.where(_ssd_keep(forward), 1.0, 0.0).astype(BF16)
    dtc = _softplus(dt_raw + bias)
    a = dtc * (a_neg * LOG2E)
    hi, mid, lo = _split3(a)
    cum = (jnp.dot(tri, hi, preferred_element_type=F32)
           + jnp.dot(tri, mid, preferred_element_type=F32)
           + jnp.dot(tri, lo, preferred_element_type=F32))
    pre_ref[0] = cum
    pre_ref[1] = cum.T - jnp.log2(dtc.T)
    pre_ref[2] = lax.dot_general(cm.astype(BF16), bm.astype(BF16), _NT,
                                 preferred_element_type=F32)
    pre_ref[3] = bm.T


def _ssd_main(xs, cm, pre_ref, h_ref, y_ref, row0, lane0, forward, dskip):
    Q = SSD_CHUNK
    keep = _ssd_keep(forward)
    last = Q - 1 if forward else 0
    cum = pre_ref[0]
    row_t = pre_ref[1]
    gm = pre_ref[2]
    bt = pre_ref[3]
    lane = lax.broadcasted_iota(jnp.int32, (Q, LANES), 1)
    first_half = lane < SSD_HEAD_DIM

    for pair in range(SSD_HEADS_PER_GROUP // 2):
        cols = slice(pair * LANES, (pair + 1) * LANES)
        xs_pair = xs[:, cols]
        h_pair = h_ref[:, cols]
        y_pair = None
        s_pair = None
        decs = []
        for e in range(2):
            ln = lane0 + 2 * pair + e
            hmask = first_half if e == 0 else jnp.logical_not(first_half)
            colb = jnp.broadcast_to(cum[:, ln:ln + 1], (Q, Q))
            rowb = jnp.broadcast_to(row_t[ln:ln + 1, :], (Q, Q))
            decay = jnp.where(keep, jnp.exp2(colb - rowb), 0.0)
            m1 = (gm * decay).astype(BF16)
            m2 = (cm * jnp.exp2(colb)).astype(BF16)
            lhs = jnp.concatenate([m1, m2], axis=1)
            xr = jnp.where(hmask, xs_pair, 0.0).astype(BF16)
            hr = jnp.where(hmask, h_pair, 0.0).astype(BF16)
            rhs = jnp.concatenate([xr, hr], axis=0)
            yc = jnp.dot(lhs, rhs, preferred_element_type=F32)
            y_pair = yc if y_pair is None else y_pair + yc
            tot = colb[last:last + 1, :]
            wb = (bt * jnp.exp2(tot - rowb)).astype(BF16)
            sc = jnp.dot(wb, xr, preferred_element_type=F32)
            s_pair = sc if s_pair is None else s_pair + sc
            decs.append(jnp.exp2(tot))
        dec = jnp.where(first_half[0:1, :], decs[0], decs[1])
        h_ref[:, cols] = h_pair * dec + s_pair
        if dskip is not None:
            y_pair = y_pair + dskip[:, cols] * xs_pair
        y_ref[pl.ds(row0, Q), cols] = y_pair


def _ssd_kernel(xsf_ref, bf_ref, cf_ref, dtf_ref, xsb_ref, bb_ref, cb_ref, dtb_ref,
                bias_ref, alog_ref, dsk_ref, yf_ref, yb_ref, hf_ref, hb_ref, pre_ref, *, nck):
    @pl.when(pl.program_id(2) == 0)
    def _():
        hf_ref[...] = jnp.zeros_like(hf_ref)
        hb_ref[...] = jnp.zeros_like(hb_ref)

    bias = bias_ref[...]
    a_neg = -jnp.exp(alog_ref[...])
    dsk = dsk_ref[...]
    Q = SSD_CHUNK

    def rows(j):
        if isinstance(j, int):
            return j * Q, (nck - 1 - j) * Q
        return pl.multiple_of(j * Q, Q), pl.multiple_of((nck - 1 - j) * Q, Q)

    def prepare(j, slot):
        rf, rb = rows(j)
        _ssd_prepare(bf_ref[pl.ds(rf, Q), :], cf_ref[pl.ds(rf, Q), :], dtf_ref[pl.ds(rf, Q), :],
                     bias, a_neg, True, pre_ref.at[slot, 0])
        _ssd_prepare(bb_ref[pl.ds(rb, Q), :], cb_ref[pl.ds(rb, Q), :], dtb_ref[pl.ds(rb, Q), :],
                     bias, a_neg, False, pre_ref.at[slot, 1])

    def main(j, slot):
        rf, rb = rows(j)
        _ssd_main(xsf_ref[pl.ds(rf, Q), :], cf_ref[pl.ds(rf, Q), :], pre_ref.at[slot, 0],
                  hf_ref, yf_ref, rf, 0, True, dsk)
        _ssd_main(xsb_ref[pl.ds(rb, Q), :], cb_ref[pl.ds(rb, Q), :], pre_ref.at[slot, 1],
                  hb_ref, yb_ref, rb, SSD_HEADS_PER_GROUP, False, None)

    prepare(0, 0)

    def body(t, carry):
        for u in range(SSD_STEPS_PER_TRIP):
            j = SSD_STEPS_PER_TRIP * t + u
            prepare(jnp.minimum(j + 1, nck - 1), (u + 1) % 2)
            main(j, u % 2)
        return carry

    lax.fori_loop(0, nck // SSD_STEPS_PER_TRIP, body, 0)


def _ssd(xc, dt, bias, alog, dskip, batch, seq, rb):
    T = xc.shape[0]
    rb = min(rb, seq)
    nb = seq // rb
    nck = rb // SSD_CHUNK
    assert nck % SSD_STEPS_PER_TRIP == 0 and nb * rb == seq
    gw = SSD_HEADS_PER_GROUP * SSD_HEAD_DIM
    b_off = SSD_INNER // LANES
    c_off = b_off + SSD_GROUPS

    def fwd(col):
        return lambda b, g, i: (b * nb + i, col(g))

    def bwd(col):
        return lambda b, g, i: (b * nb + nb - 1 - i, col(g))

    def specs(mk):
        return [pl.BlockSpec((rb, gw), mk(lambda g: g)),
                pl.BlockSpec((rb, LANES), mk(lambda g: b_off + g)),
                pl.BlockSpec((rb, LANES), mk(lambda g: c_off + g)),
                pl.BlockSpec((rb, LANES), mk(lambda g: g))]

    out_sds = jax.ShapeDtypeStruct((T, SSD_INNER), F32)
    return pl.pallas_call(
        functools.partial(_ssd_kernel, nck=nck),
        grid=(batch, SSD_GROUPS, nb),
        in_specs=specs(fwd) + specs(bwd) + [
            pl.BlockSpec((None, 1, LANES), lambda b, g, i: (g, 0, 0)),
            pl.BlockSpec((None, 1, LANES), lambda b, g, i: (g, 0, 0)),
            pl.BlockSpec((1, gw), lambda b, g, i: (0, g))],
        out_specs=[pl.BlockSpec((rb, gw), fwd(lambda g: g)),
                   pl.BlockSpec((rb, gw), bwd(lambda g: g))],
        out_shape=[out_sds, out_sds],
        scratch_shapes=[pltpu.VMEM((SSD_STATE, gw), F32), pltpu.VMEM((SSD_STATE, gw), F32),
                        pltpu.VMEM((2, 2, 4, SSD_CHUNK, LANES), F32)],
        compiler_params=_cparams(("parallel", "parallel", "arbitrary")),
        name="ssd_scan",
    )(xc, xc, xc, dt, xc, xc, xc, dt, bias, alog, dskip)


def _attn_kernel(lam_ref, q_ref, k_ref, vt_ref, dg_ref, nw_ref, o_ref,
                 qm_ref, s_ref, bm_ref, m_ref, acc_ref, *, tk, nk, out_scale):
    q = q_ref[...]
    lane = lax.broadcasted_iota(jnp.int32, q.shape, 1)
    map0 = (lane % SSD_HEAD_DIM) < (DIFF_HEAD_DIM // 2)
    zero = jnp.zeros_like(q)
    qm_ref[0] = jnp.where(map0, q, zero)
    qm_ref[1] = jnp.where(map0, zero, q)
    m_ref[...] = jnp.full_like(m_ref, -jnp.inf)
    acc_ref[...] = jnp.zeros_like(acc_ref)
    ones = jnp.ones((ONES_ROWS, tk), BF16)
    dv = 2 * DIFF_HEAD_DIM

    def key_start(blk):
        return blk * tk if isinstance(blk, int) else pl.multiple_of(blk * tk, tk)

    def scores(blk, slot):
        k = k_ref[pl.ds(key_start(blk), tk), :]
        for c in range(2):
            s = lax.dot_general(k, qm_ref[c], _NT, preferred_element_type=F32)
            s_ref[slot, c] = s
            bm_ref[slot, c] = jnp.max(s, axis=0, keepdims=True)

    def softmax_pv(blk, slot):
        vt = vt_ref[:, pl.ds(key_start(blk), tk)]
        lhs = jnp.concatenate([vt, ones], axis=0)
        for c in range(2):
            s = s_ref[slot, c]
            m_old = m_ref[c]
            m_new = jnp.maximum(m_old, bm_ref[slot, c])
            alpha = jnp.exp2(m_old - m_new)
            p = jnp.exp2(s - m_new).astype(BF16)
            acc_ref[c] = acc_ref[c] * alpha + jnp.dot(lhs, p, preferred_element_type=F32)
            m_ref[c] = m_new

    scores(0, 0)

    def pair(p):
        scores(2 * p + 1, 1)
        softmax_pv(2 * p, 0)
        scores(2 * p + 2, 0)
        softmax_pv(2 * p + 1, 1)

    n_pairs = nk // 2 - 1
    trips = n_pairs // PAIRS_PER_TRIP

    def body(t, carry):
        for u in range(PAIRS_PER_TRIP):
            pair(PAIRS_PER_TRIP * t + u)
        return carry

    lax.fori_loop(0, trips, body, 0)
    for p in range(trips * PAIRS_PER_TRIP, n_pairs):
        pair(p)
    scores(nk - 1, 1)
    softmax_pv(nk - 2, 0)
    softmax_pv(nk - 1, 1)

    lam = lam_ref[0, 0]
    o = (acc_ref[0, :dv, :] * (1.0 / acc_ref[0, dv:dv + 1, :])
         - lam * (acc_ref[1, :dv, :] * (1.0 / acc_ref[1, dv:dv + 1, :])))
    ms = jnp.mean(o * o, axis=0, keepdims=True)
    o = o * lax.rsqrt(ms + NORM_EPS)
    ot = o.T * (nw_ref[...] * out_scale)
    o_ref[...] = (ot * _silu(dg_ref[...])).astype(o_ref.dtype)


def _diff_attention(lam, q, k, vt, dg, norm_w, batch, seq, out_scale, tq, tk):
    T = q.shape[0]
    tq, tk = min(tq, seq), min(tk, seq // 2)
    nq = seq // tq
    nk = seq // tk
    assert nk % 2 == 0 and nk * tk == seq
    return pl.pallas_call(
        functools.partial(_attn_kernel, tk=tk, nk=nk, out_scale=out_scale),
        grid=(batch, DIFF_HEADS, nq),
        in_specs=[pl.BlockSpec(memory_space=pltpu.SMEM),
                  pl.BlockSpec((tq, LANES), lambda b, h, i: (b * nq + i, h)),
                  pl.BlockSpec((seq, LANES), lambda b, h, i: (b, h)),
                  pl.BlockSpec((None, LANES, seq), lambda b, h, i: (b, h, 0)),
                  pl.BlockSpec((tq, LANES), lambda b, h, i: (b * nq + i, h)),
                  pl.BlockSpec((1, LANES), lambda b, h, i: (0, 0))],
        out_specs=pl.BlockSpec((tq, LANES), lambda b, h, i: (b * nq + i, h)),
        out_shape=jax.ShapeDtypeStruct((T, DIFF_WIDTH), BF16),
        scratch_shapes=[pltpu.VMEM((2, tq, LANES), BF16),
                        pltpu.VMEM((2, 2, tk, tq), F32),
                        pltpu.VMEM((2, 2, 1, tq), F32),
                        pltpu.VMEM((2, 1, tq), F32),
                        pltpu.VMEM((2, LANES + ONES_ROWS, tq), F32)],
        compiler_params=_cparams(("parallel", "parallel", "parallel")),
        name="diff_attention",
    )(lam, q, k, vt, dg, norm_w)


def _tail_kernel(x_ref, xb_ref, yf_ref, yb_ref, yd_ref, kv_ref,
                 wz_ref, wcq_ref, wcg_ref, wgl_ref, wbs_ref, wbd_ref, wbc_ref, wo_ref,
                 nw_ref, gb_ref, lng_ref, lnb_ref, xo_ref, xbo_ref):
    xb = xb_ref[...]

    def gate(idx):
        cols = slice(idx * D_MODEL, (idx + 1) * D_MODEL)
        gl = jnp.dot(xb, wgl_ref[:, cols], preferred_element_type=F32) + gb_ref[:, cols]
        return _sigmoid(gl)

    z = jnp.dot(xb, wz_ref[...], preferred_element_type=F32)
    y = (yf_ref[...] + yb_ref[...]) * _silu(z)
    y = y * lax.rsqrt(jnp.mean(y * y, axis=-1, keepdims=True) + NORM_EPS) * nw_ref[...]
    merged = gate(0) * jnp.dot(y.astype(BF16), wbs_ref[...], preferred_element_type=F32)

    merged = merged + gate(1) * jnp.dot(yd_ref[...], wbd_ref[...], preferred_element_type=F32)

    cq = (jnp.dot(xb, wcq_ref[...], preferred_element_type=F32)
          * (CROSS_HEAD_DIM ** -0.5)).astype(BF16)
    cg = jnp.dot(xb, wcg_ref[...], preferred_element_type=F32)
    outs = []
    for h in range(CROSS_HEADS):
        cols = slice(h * CROSS_HEAD_DIM, (h + 1) * CROSS_HEAD_DIM)
        mk = kv_ref[:, cols]
        mv = kv_ref[:, CROSS_WIDTH + h * CROSS_HEAD_DIM:CROSS_WIDTH + (h + 1) * CROSS_HEAD_DIM]
        s = lax.dot_general(cq[:, cols], mk, _NT, preferred_element_type=F32)
        e = jnp.exp(s - jnp.max(s, axis=-1, keepdims=True))
        p = e * (1.0 / jnp.sum(e, axis=-1, keepdims=True))
        outs.append(jnp.dot(p.astype(BF16), mv, preferred_element_type=F32))
    yc = jnp.concatenate(outs, axis=1) * _silu(cg)
    merged = merged + gate(2) * jnp.dot(yc.astype(BF16), wbc_ref[...],
                                        preferred_element_type=F32)

    out = jnp.dot(merged.astype(BF16), wo_ref[...], preferred_element_type=F32)
    r = DEEPNORM_ALPHA * x_ref[...] + out
    mu = jnp.mean(r, axis=-1, keepdims=True)
    d = r - mu
    var = jnp.mean(d * d, axis=-1, keepdims=True)
    xn = d * lax.rsqrt(var + NORM_EPS) * lng_ref[...] + lnb_ref[...]
    xo_ref[...] = xn
    xbo_ref[...] = xn.astype(BF16)


def _tail(x, xb, yf, yb, yd, kv, wz, wcq, wcg, wgl, wbs, wbd, wbc, wo, nw, gb, lng, lnb,
          seq, tm):
    T = x.shape[0]
    tm = min(tm, seq)
    tps = seq // tm
    mem_tokens = kv.shape[0] // (T // seq)
    row = lambda w: pl.BlockSpec((tm, w), lambda i: (i, 0))
    const = lambda a: pl.BlockSpec(a.shape, lambda i: (0, 0), pipeline_mode=pl.Buffered(1))
    return pl.pallas_call(
        _tail_kernel,
        grid=(T // tm,),
        in_specs=[row(D_MODEL), row(D_MODEL), row(SSD_INNER), row(SSD_INNER), row(DIFF_WIDTH),
                  pl.BlockSpec((mem_tokens, 2 * CROSS_WIDTH), lambda i: (i // tps, 0)),
                  const(wz), const(wcq), const(wcg), const(wgl), const(wbs), const(wbd),
                  const(wbc), const(wo), const(nw), const(gb), const(lng), const(lnb)],
        out_specs=[row(D_MODEL), row(D_MODEL)],
        out_shape=[jax.ShapeDtypeStruct((T, D_MODEL), F32),
                   jax.ShapeDtypeStruct((T, D_MODEL), BF16)],
        compiler_params=_cparams(("parallel",)),
        name="tail",
    )(x, xb, yf, yb, yd, kv, wz, wcq, wcg, wgl, wbs, wbd, wbc, wo, nw, gb, lng, lnb)


def _in_splits():
    sizes = [SSD_INNER, SSD_CONV_CH, 2 * SSD_HEADS, DIFF_WIDTH, DIFF_WIDTH, DIFF_WIDTH,
             DIFF_WIDTH, CROSS_WIDTH, CROSS_WIDTH, N_BRANCH * D_MODEL]
    offs = np.concatenate([[0], np.cumsum(sizes)])
    return [(int(offs[i]), int(offs[i + 1])) for i in range(len(sizes))]


def _rope_perm():
    half = DIFF_HEAD_DIM // 2
    perm = np.zeros(DIFF_WIDTH, np.int32)
    for h in range(DIFF_HEADS):
        for l in range(LANES):
            hf, c, j = l // 64, (l % 64) // half, l % half
            perm[h * LANES + l] = h * LANES + c * DIFF_HEAD_DIM + hf * half + j
    return perm


def _dt_placement():
    cols = np.zeros(2 * SSD_HEADS, np.int32)
    for d in range(2):
        for g in range(SSD_GROUPS):
            for r in range(SSD_HEADS_PER_GROUP):
                cols[d * SSD_HEADS + g * SSD_HEADS_PER_GROUP + r] = (
                    g * LANES + d * SSD_HEADS_PER_GROUP + r)
    return cols


def _per_group_lanes(v):
    out = jnp.zeros((SSD_GROUPS * LANES,), F32).at[_dt_placement()].set(v.reshape(-1).astype(F32))
    return out.reshape(SSD_GROUPS, 1, LANES)


def kernel(x, mem, positions, w_in, conv_w, conv_b, dt_bias, a_log, d_skip, ssd_norm_w,
           diff_lam, diff_norm_w, w_mem_kv, w_br_ssd, w_br_diff, w_br_cross, gate_b, w_out,
           ln_g, ln_b):
    batch, seq, _ = x.shape
    T = batch * seq
    mem_tokens = mem.shape[1]
    (sz, sxbc, sdt, sdq, sdk, sdv, sdg, scq, scg, sgl) = _in_splits()
    perm = _rope_perm()
    dt_cols = _dt_placement()

    inv = 1.0 / (ROPE_THETA ** (jnp.arange(0, DIFF_HEAD_DIM, 2, dtype=F32) / DIFF_HEAD_DIM))
    inv_t = jnp.tile(inv, LANES // inv.shape[0])[None, :]
    sgn_t = jnp.where(jnp.arange(LANES) < LANES // 2, -1.0, 1.0).astype(F32)[None, :]
    pos = positions.reshape(T, 1)

    xf = x.reshape(T, D_MODEL)
    xb = xf.astype(BF16)
    memb = mem.reshape(batch * mem_tokens, D_MODEL).astype(BF16)

    for layer in range(DEPTH):
        lambda_init = 0.8 - 0.6 * math.exp(-0.3 * layer)
        w = w_in[layer]
        wb = lambda s: w[:, s[0]:s[1]].astype(BF16)
        w_q = wb(sdq)[:, perm]
        w_k = wb(sdk)[:, perm]
        w_vt = wb(sdv).T
        w_dt = jnp.zeros((D_MODEL, SSD_GROUPS * LANES), BF16).at[:, dt_cols].set(wb(sdt))
        conv_w8 = jnp.zeros((8, SSD_CONV_CH), F32).at[:SSD_CONV].set(conv_w[layer])

        xc = _proj_conv(xb, wb(sxbc), conv_w8, conv_b[layer][None, :], seq, F32, 512, 1024)
        q, k, vt, dg, dt = _proj_attn(xb, w_q, w_k, w_vt, wb(sdg), w_dt, pos, inv_t, sgn_t,
                                      DIFF_HEAD_DIM ** -0.5 * LOG2E, batch, seq, 512)
        kv = _proj_plain(memb, w_mem_kv[layer].astype(BF16), BF16, 1024, 1024)

        yf, yb = _ssd(xc, dt, _per_group_lanes(dt_bias[layer]), _per_group_lanes(a_log[layer]),
                      jnp.repeat(d_skip[layer].astype(F32), SSD_HEAD_DIM)[None, :],
                      batch, seq, 2048)

        lq = diff_lam[layer].astype(F32)
        lam = (jnp.exp(jnp.sum(lq[0] * lq[1])) - jnp.exp(jnp.sum(lq[2] * lq[3]))
               + lambda_init).reshape(1, 1)
        yd = _diff_attention(lam, q, k, vt, dg, diff_norm_w[layer].astype(F32)[None, :],
                             batch, seq, 1.0 - lambda_init, 512, 512)

        xf, xb = _tail(xf, xb, yf, yb, yd, kv,
                       wb(sz), wb(scq), wb(scg), wb(sgl),
                       w_br_ssd[layer].astype(BF16), w_br_diff[layer].astype(BF16),
                       w_br_cross[layer].astype(BF16), w_out[layer].astype(BF16),
                       ssd_norm_w[layer].astype(F32)[None, :],
                       gate_b[layer].astype(F32).reshape(1, N_BRANCH * D_MODEL),
                       ln_g[layer].astype(F32)[None, :], ln_b[layer].astype(F32)[None, :],
                       seq, 256)

    return xf.reshape(batch, seq, D_MODEL).astype(x.dtype)
```

```python
import functools
import math

import numpy as np
import jax
import jax.numpy as jnp
from jax import lax
from jax.experimental import pallas as pl
from jax.experimental.pallas import tpu as pltpu

F32 = jnp.float32
BF16 = jnp.bfloat16

D_MODEL = 1024
DEPTH = 2
SSD_INNER = 2048
SSD_HEAD_DIM = 64
SSD_HEADS = 32
SSD_GROUPS = 4
SSD_HEADS_PER_GROUP = SSD_HEADS // SSD_GROUPS
SSD_STATE = 128
SSD_CONV = 5
SSD_CHUNK = 128
SSD_CONV_CH = SSD_INNER + 2 * SSD_GROUPS * SSD_STATE
DIFF_HEAD_DIM = 64
DIFF_HEADS = 8
DIFF_WIDTH = 1024
ROPE_THETA = 10000.0
CROSS_HEADS = 4
CROSS_HEAD_DIM = 256
CROSS_WIDTH = 1024
N_BRANCH = 3
DEEPNORM_ALPHA = (2 * DEPTH) ** 0.25
NORM_EPS = 1e-5

LANES = 128
HALO = 16
ROW_PHASES = 4
ONES_ROWS = 16
PAIRS_PER_TRIP = 3
SSD_STEPS_PER_TRIP = 2
LOG2E = 1.4426950408889634
VMEM_LIMIT = 56 * 1024 * 1024

_NT = (((1,), (1,)), ((), ()))


def _sigmoid(v):
    return 1.0 / (1.0 + jnp.exp(-v))


def _silu(v):
    return v * _sigmoid(v)


def _cparams(sem):
    return pltpu.CompilerParams(dimension_semantics=sem, vmem_limit_bytes=VMEM_LIMIT)


def _proj_plain_kernel(x_ref, w_ref, o_ref):
    o_ref[...] = jnp.dot(x_ref[...], w_ref[...],
                         preferred_element_type=F32).astype(o_ref.dtype)


def _proj_plain(xb, w, out_dtype, tm, tn):
    T, K = xb.shape
    N = w.shape[1]
    tm, tn = min(tm, T), min(tn, N)
    return pl.pallas_call(
        _proj_plain_kernel,
        grid=(N // tn, T // tm),
        in_specs=[pl.BlockSpec((tm, K), lambda j, i: (i, 0)),
                  pl.BlockSpec((K, tn), lambda j, i: (0, j))],
        out_specs=pl.BlockSpec((tm, tn), lambda j, i: (i, j)),
        out_shape=jax.ShapeDtypeStruct((T, N), out_dtype),
        compiler_params=_cparams(("parallel", "parallel")),
        name="proj_plain",
    )(xb, w)


def _proj_attn_kernel(x_ref, wq_ref, wk_ref, wvt_ref, wdg_ref, wdt_ref, pos_ref, inv_ref, sgn_ref,
                      q_ref, k_ref, vt_ref, dg_ref, dt_ref, *, q_scale):
    x = x_ref[...]
    ang = pos_ref[...].astype(F32) * inv_ref[...]
    c = jnp.cos(ang)
    s = jnp.sin(ang) * sgn_ref[...]

    def rope(w_ref, o_ref, scale):
        acc = jnp.dot(x, w_ref[...], preferred_element_type=F32)
        for j in range(acc.shape[1] // LANES):
            t = acc[:, j * LANES:(j + 1) * LANES]
            o = t * c + pltpu.roll(t, LANES // 2, axis=1) * s
            if scale != 1.0:
                o = o * scale
            o_ref[:, j * LANES:(j + 1) * LANES] = o.astype(o_ref.dtype)

    rope(wq_ref, q_ref, q_scale)
    rope(wk_ref, k_ref, 1.0)
    vt_ref[...] = lax.dot_general(wvt_ref[...], x, _NT,
                                  preferred_element_type=F32).astype(vt_ref.dtype)
    dg_ref[...] = jnp.dot(x, wdg_ref[...], preferred_element_type=F32)
    dt_ref[...] = jnp.dot(x, wdt_ref[...], preferred_element_type=F32)


def _proj_attn(xb, w_q, w_k, w_vt, w_dg, w_dt, pos, inv_t, sgn_t, q_scale, batch, seq, tm):
    T, K = xb.shape
    tm = min(tm, seq)
    ns = seq // tm
    row = lambda n: pl.BlockSpec((tm, n), lambda i: (i, 0))
    const = lambda a: pl.BlockSpec(a.shape, lambda i: (0, 0), pipeline_mode=pl.Buffered(1))
    n_v, n_dt = w_vt.shape[0], w_dt.shape[1]
    return pl.pallas_call(
        functools.partial(_proj_attn_kernel, q_scale=q_scale),
        grid=(T // tm,),
        in_specs=[row(K), const(w_q), const(w_k), const(w_vt), const(w_dg), const(w_dt),
                  row(1), const(inv_t), const(sgn_t)],
        out_specs=[row(DIFF_WIDTH), row(DIFF_WIDTH),
                   pl.BlockSpec((None, n_v, tm), lambda i: (i // ns, 0, i % ns)),
                   row(DIFF_WIDTH), row(n_dt)],
        out_shape=[jax.ShapeDtypeStruct((T, DIFF_WIDTH), BF16),
                   jax.ShapeDtypeStruct((T, DIFF_WIDTH), BF16),
                   jax.ShapeDtypeStruct((batch, n_v, seq), BF16),
                   jax.ShapeDtypeStruct((T, DIFF_WIDTH), F32),
                   jax.ShapeDtypeStruct((T, n_dt), F32)],
        compiler_params=_cparams(("parallel",)),
        name="proj_attn",
    )(xb, w_q, w_k, w_vt, w_dg, w_dt, pos, inv_t, sgn_t)


def _proj_conv_kernel(xp_ref, x_ref, xn_ref, w_ref, cw_ref, cb_ref, o_ref, xcat_ref, acc_ref,
                      out_ref, *, tm, tiles_per_seq):
    i = pl.program_id(1)
    pos = i % tiles_per_seq
    xp = xp_ref[...]
    xn = xn_ref[...]
    xcat_ref[0:HALO, :] = jnp.where(pos == 0, jnp.zeros_like(xp), xp)
    xcat_ref[HALO:HALO + tm, :] = x_ref[...]
    xcat_ref[HALO + tm:, :] = jnp.where(pos == tiles_per_seq - 1, jnp.zeros_like(xn), xn)
    acc = jnp.dot(xcat_ref[...], w_ref[...], preferred_element_type=F32)
    n_slabs = acc.shape[1] // LANES
    for s in range(n_slabs):
        acc_ref[s] = acc[:, s * LANES:(s + 1) * LANES]
    pad = SSD_CONV // 2
    per_phase = tm // ROW_PHASES
    for s in range(n_slabs):
        lanes = slice(s * LANES, (s + 1) * LANES)
        for r in range(ROW_PHASES):
            out = cb_ref[:, lanes]
            for j in range(SSD_CONV):
                tap = acc_ref[s, pl.ds(HALO + r + j - pad, per_phase, stride=ROW_PHASES), :]
                out = out + cw_ref[j:j + 1, lanes] * tap
            out_ref[s, pl.ds(r, per_phase, stride=ROW_PHASES), :] = _silu(out)
        o_ref[:, lanes] = out_ref[s].astype(o_ref.dtype)


def _proj_conv(xb, w, conv_w8, conv_b, seq, out_dtype, tm, tn):
    T, K = xb.shape
    N = w.shape[1]
    tm, tn = min(tm, seq), min(tn, N)
    tps = seq // tm
    hb = tm // HALO
    nhb = T // HALO
    return pl.pallas_call(
        functools.partial(_proj_conv_kernel, tm=tm, tiles_per_seq=tps),
        grid=(N // tn, T // tm),
        in_specs=[pl.BlockSpec((HALO, K), lambda j, i: (jnp.maximum(i * hb - 1, 0), 0)),
                  pl.BlockSpec((tm, K), lambda j, i: (i, 0)),
                  pl.BlockSpec((HALO, K), lambda j, i: (jnp.minimum((i + 1) * hb, nhb - 1), 0)),
                  pl.BlockSpec((K, tn), lambda j, i: (0, j)),
                  pl.BlockSpec((8, tn), lambda j, i: (0, j)),
                  pl.BlockSpec((1, tn), lambda j, i: (0, j))],
        out_specs=pl.BlockSpec((tm, tn), lambda j, i: (i, j)),
        out_shape=jax.ShapeDtypeStruct((T, N), out_dtype),
        scratch_shapes=[pltpu.VMEM((tm + 2 * HALO, K), BF16),
                        pltpu.VMEM((tn // LANES, tm + 2 * HALO, LANES), F32),
                        pltpu.VMEM((tn // LANES, tm, LANES), F32)],
        compiler_params=_cparams(("parallel", "parallel")),
        name="proj_conv",
    )(xb, xb, xb, w, conv_w8, conv_b)


def _softplus(v):
    return jnp.maximum(v, 0.0) + jnp.log(1.0 + jnp.exp(-jnp.abs(v)))


def _split3(v):
    hi = v.astype(BF16)
    r1 = v - hi.astype(F32)
    mid = r1.astype(BF16)
    lo = (r1 - mid.astype(F32)).astype(BF16)
    return hi, mid, lo


def _ssd_keep(forward):
    qi = lax.broadcasted_iota(jnp.int32, (SSD_CHUNK, SSD_CHUNK), 0)
    si = lax.broadcasted_iota(jnp.int32, (SSD_CHUNK, SSD_CHUNK), 1)
    return si <= qi if forward else si >= qi


def _ssd_prepare(bm, cm, dt_raw, bias, a_neg, forward, pre_ref):
    tri = jnp.where(_ssd_keep(forward), 1.0, 0.0).astype(BF16)
    dtc = _softplus(dt_raw + bias)
    a = dtc * (a_neg * LOG2E)
    hi, mid, lo = _split3(a)
    cum = (jnp.dot(tri, hi, preferred_element_type=F32)
           + jnp.dot(tri, mid, preferred_element_type=F32)
           + jnp.dot(tri, lo, preferred_element_type=F32))
    pre_ref[0] = cum
    pre_ref[1] = cum.T - jnp.log2(dtc.T)
    pre_ref[2] = lax.dot_general(cm.astype(BF16), bm.astype(BF16), _NT,
                                 preferred_element_type=F32)
    pre_ref[3] = bm.T


def _ssd_main(xs, cm, pre_ref, h_ref, y_ref, row0, lane0, forward, dskip):
    Q = SSD_CHUNK
    keep = _ssd_keep(forward)
    last = Q - 1 if forward else 0
    cum = pre_ref[0]
    row_t = pre_ref[1]
    gm = pre_ref[2]
    bt = pre_ref[3]
    lane = lax.broadcasted_iota(jnp.int32, (Q, LANES), 1)
    first_half = lane < SSD_HEAD_DIM

    for pair in range(SSD_HEADS_PER_GROUP // 2):
        cols = slice(pair * LANES, (pair + 1) * LANES)
        xs_pair = xs[:, cols]
        h_pair = h_ref[:, cols]
        y_pair = None
        s_pair = None
        decs = []
        for e in range(2):
            ln = lane0 + 2 * pair + e
            hmask = first_half if e == 0 else jnp.logical_not(first_half)
            colb = jnp.broadcast_to(cum[:, ln:ln + 1], (Q, Q))
            rowb = jnp.broadcast_to(row_t[ln:ln + 1, :], (Q, Q))
            decay = jnp.where(keep, jnp.exp2(colb - rowb), 0.0)
            m1 = (gm * decay).astype(BF16)
            m2 = (cm * jnp.exp2(colb)).astype(BF16)
            lhs = jnp.concatenate([m1, m2], axis=1)
            xr = jnp.where(hmask, xs_pair, 0.0).astype(BF16)
            hr = jnp.where(hmask, h_pair, 0.0).astype(BF16)
            rhs = jnp.concatenate([xr, hr], axis=0)
            yc = jnp.dot(lhs, rhs, preferred_element_type=F32)
            y_pair = yc if y_pair is None else y_pair + yc
            tot = colb[last:last + 1, :]
            wb = (bt * jnp.exp2(tot - rowb)).astype(BF16)
            sc = jnp.dot(wb, xr, preferred_element_type=F32)
            s_pair = sc if s_pair is None else s_pair + sc
            decs.append(jnp.exp2(tot))
        dec = jnp.where(first_half[0:1, :], decs[0], decs[1])
        h_ref[:, cols] = h_pair * dec + s_pair
        if dskip is not None:
            y_pair = y_pair + dskip[:, cols] * xs_pair
        y_ref[pl.ds(row0, Q), cols] = y_pair


def _ssd_kernel(xsf_ref, bf_ref, cf_ref, dtf_ref, xsb_ref, bb_ref, cb_ref, dtb_ref,
                bias_ref, alog_ref, dsk_ref, yf_ref, yb_ref, hf_ref, hb_ref, pre_ref, *, nck):
    @pl.when(pl.program_id(2) == 0)
    def _():
        hf_ref[...] = jnp.zeros_like(hf_ref)
        hb_ref[...] = jnp.zeros_like(hb_ref)

    bias = bias_ref[...]
    a_neg = -jnp.exp(alog_ref[...])
    dsk = dsk_ref[...]
    Q = SSD_CHUNK

    def rows(j):
        if isinstance(j, int):
            return j * Q, (nck - 1 - j) * Q
        return pl.multiple_of(j * Q, Q), pl.multiple_of((nck - 1 - j) * Q, Q)

    def prepare(j, slot):
        rf, rb = rows(j)
        _ssd_prepare(bf_ref[pl.ds(rf, Q), :], cf_ref[pl.ds(rf, Q), :], dtf_ref[pl.ds(rf, Q), :],
                     bias, a_neg, True, pre_ref.at[slot, 0])
        _ssd_prepare(bb_ref[pl.ds(rb, Q), :], cb_ref[pl.ds(rb, Q), :], dtb_ref[pl.ds(rb, Q), :],
                     bias, a_neg, False, pre_ref.at[slot, 1])

    def main(j, slot):
        rf, rb = rows(j)
        _ssd_main(xsf_ref[pl.ds(rf, Q), :], cf_ref[pl.ds(rf, Q), :], pre_ref.at[slot, 0],
                  hf_ref, yf_ref, rf, 0, True, dsk)
        _ssd_main(xsb_ref[pl.ds(rb, Q), :], cb_ref[pl.ds(rb, Q), :], pre_ref.at[slot, 1],
                  hb_ref, yb_ref, rb, SSD_HEADS_PER_GROUP, False, None)

    prepare(0, 0)

    def body(t, carry):
        for u in range(SSD_STEPS_PER_TRIP):
            j = SSD_STEPS_PER_TRIP * t + u
            prepare(jnp.minimum(j + 1, nck - 1), (u + 1) % 2)
            main(j, u % 2)
        return carry

    lax.fori_loop(0, nck // SSD_STEPS_PER_TRIP, body, 0)


def _ssd(xc, dt, bias, alog, dskip, batch, seq, rb):
    T = xc.shape[0]
    rb = min(rb, seq)
    nb = seq // rb
    nck = rb // SSD_CHUNK
    assert nck % SSD_STEPS_PER_TRIP == 0 and nb * rb == seq
    gw = SSD_HEADS_PER_GROUP * SSD_HEAD_DIM
    b_off = SSD_INNER // LANES
    c_off = b_off + SSD_GROUPS

    def fwd(col):
        return lambda b, g, i: (b * nb + i, col(g))

    def bwd(col):
        return lambda b, g, i: (b * nb + nb - 1 - i, col(g))

    def specs(mk):
        return [pl.BlockSpec((rb, gw), mk(lambda g: g)),
                pl.BlockSpec((rb, LANES), mk(lambda g: b_off + g)),
                pl.BlockSpec((rb, LANES), mk(lambda g: c_off + g)),
                pl.BlockSpec((rb, LANES), mk(lambda g: g))]

    out_sds = jax.ShapeDtypeStruct((T, SSD_INNER), F32)
    return pl.pallas_call(
        functools.partial(_ssd_kernel, nck=nck),
        grid=(batch, SSD_GROUPS, nb),
        in_specs=specs(fwd) + specs(bwd) + [
            pl.BlockSpec((None, 1, LANES), lambda b, g, i: (g, 0, 0)),
            pl.BlockSpec((None, 1, LANES), lambda b, g, i: (g, 0, 0)),
            pl.BlockSpec((1, gw), lambda b, g, i: (0, g))],
        out_specs=[pl.BlockSpec((rb, gw), fwd(lambda g: g)),
                   pl.BlockSpec((rb, gw), bwd(lambda g: g))],
        out_shape=[out_sds, out_sds],
        scratch_shapes=[pltpu.VMEM((SSD_STATE, gw), F32), pltpu.VMEM((SSD_STATE, gw), F32),
                        pltpu.VMEM((2, 2, 4, SSD_CHUNK, LANES), F32)],
        compiler_params=_cparams(("parallel", "parallel", "arbitrary")),
        name="ssd_scan",
    )(xc, xc, xc, dt, xc, xc, xc, dt, bias, alog, dskip)


def _attn_kernel(lam_ref, q_ref, k_ref, vt_ref, dg_ref, nw_ref, o_ref,
                 qm_ref, s_ref, bm_ref, m_ref, acc_ref, *, tk, nk, out_scale):
    q = q_ref[...]
    lane = lax.broadcasted_iota(jnp.int32, q.shape, 1)
    map0 = (lane % SSD_HEAD_DIM) < (DIFF_HEAD_DIM // 2)
    zero = jnp.zeros_like(q)
    qm_ref[0] = jnp.where(map0, q, zero)
    qm_ref[1] = jnp.where(map0, zero, q)
    m_ref[...] = jnp.full_like(m_ref, -jnp.inf)
    acc_ref[...] = jnp.zeros_like(acc_ref)
    ones = jnp.ones((ONES_ROWS, tk), BF16)
    dv = 2 * DIFF_HEAD_DIM

    def key_start(blk):
        return blk * tk if isinstance(blk, int) else pl.multiple_of(blk * tk, tk)

    def scores(blk, slot):
        k = k_ref[pl.ds(key_start(blk), tk), :]
        for c in range(2):
            s = lax.dot_general(k, qm_ref[c], _NT, preferred_element_type=F32)
            s_ref[slot, c] = s
            bm_ref[slot, c] = jnp.max(s, axis=0, keepdims=True)

    def softmax_pv(blk, slot):
        vt = vt_ref[:, pl.ds(key_start(blk), tk)]
        lhs = jnp.concatenate([vt, ones], axis=0)
        for c in range(2):
            s = s_ref[slot, c]
            m_old = m_ref[c]
            m_new = jnp.maximum(m_old, bm_ref[slot, c])
            alpha = jnp.exp2(m_old - m_new)
            p = jnp.exp2(s - m_new).astype(BF16)
            acc_ref[c] = acc_ref[c] * alpha + jnp.dot(lhs, p, preferred_element_type=F32)
            m_ref[c] = m_new

    scores(0, 0)

    def pair(p):
        scores(2 * p + 1, 1)
        softmax_pv(2 * p, 0)
        scores(2 * p + 2, 0)
        softmax_pv(2 * p + 1, 1)

    n_pairs = nk // 2 - 1
    trips = n_pairs // PAIRS_PER_TRIP

    def body(t, carry):
        for u in range(PAIRS_PER_TRIP):
            pair(PAIRS_PER_TRIP * t + u)
        return carry

    lax.fori_loop(0, trips, body, 0)
    for p in range(trips * PAIRS_PER_TRIP, n_pairs):
        pair(p)
    scores(nk - 1, 1)
    softmax_pv(nk - 2, 0)
    softmax_pv(nk - 1, 1)

    lam = lam_ref[0, 0]
    o = (acc_ref[0, :dv, :] * (1.0 / acc_ref[0, dv:dv + 1, :])
         - lam * (acc_ref[1, :dv, :] * (1.0 / acc_ref[1, dv:dv + 1, :])))
    ms = jnp.mean(o * o, axis=0, keepdims=True)
    o = o * lax.rsqrt(ms + NORM_EPS)
    ot = o.T * (nw_ref[...] * out_scale)
    o_ref[...] = (ot * _silu(dg_ref[...])).astype(o_ref.dtype)


def _diff_attention(lam, q, k, vt, dg, norm_w, batch, seq, out_scale, tq, tk):
    T = q.shape[0]
    tq, tk = min(tq, seq), min(tk, seq // 2)
    nq = seq // tq
    nk = seq // tk
    assert nk % 2 == 0 and nk * tk == seq
    return pl.pallas_call(
        functools.partial(_attn_kernel, tk=tk, nk=nk, out_scale=out_scale),
        grid=(batch, DIFF_HEADS, nq),
        in_specs=[pl.BlockSpec(memory_space=pltpu.SMEM),
                  pl.BlockSpec((tq, LANES), lambda b, h, i: (b * nq + i, h)),
                  pl.BlockSpec((seq, LANES), lambda b, h, i: (b, h)),
                  pl.BlockSpec((None, LANES, seq), lambda b, h, i: (b, h, 0)),
                  pl.BlockSpec((tq, LANES), lambda b, h, i: (b * nq + i, h)),
                  pl.BlockSpec((1, LANES), lambda b, h, i: (0, 0))],
        out_specs=pl.BlockSpec((tq, LANES), lambda b, h, i: (b * nq + i, h)),
        out_shape=jax.ShapeDtypeStruct((T, DIFF_WIDTH), BF16),
        scratch_shapes=[pltpu.VMEM((2, tq, LANES), BF16),
                        pltpu.VMEM((2, 2, tk, tq), F32),
                        pltpu.VMEM((2, 2, 1, tq), F32),
                        pltpu.VMEM((2, 1, tq), F32),
                        pltpu.VMEM((2, LANES + ONES_ROWS, tq), F32)],
        compiler_params=_cparams(("parallel", "parallel", "parallel")),
        name="diff_attention",
    )(lam, q, k, vt, dg, norm_w)


def _tail_kernel(x_ref, xb_ref, yf_ref, yb_ref, yd_ref, kv_ref,
                 wz_ref, wcq_ref, wcg_ref, wgl_ref, wbs_ref, wbd_ref, wbc_ref, wo_ref,
                 nw_ref, gb_ref, lng_ref, lnb_ref, xo_ref, xbo_ref):
    xb = xb_ref[...]

    def gate(idx):
        cols = slice(idx * D_MODEL, (idx + 1) * D_MODEL)
        gl = jnp.dot(xb, wgl_ref[:, cols], preferred_element_type=F32) + gb_ref[:, cols]
        return _sigmoid(gl)

    z = jnp.dot(xb, wz_ref[...], preferred_element_type=F32)
    y = (yf_ref[...] + yb_ref[...]) * _silu(z)
    y = y * lax.rsqrt(jnp.mean(y * y, axis=-1, keepdims=True) + NORM_EPS) * nw_ref[...]
    merged = gate(0) * jnp.dot(y.astype(BF16), wbs_ref[...], preferred_element_type=F32)

    merged = merged + gate(1) * jnp.dot(yd_ref[...], wbd_ref[...], preferred_element_type=F32)

    cq = (jnp.dot(xb, wcq_ref[...], preferred_element_type=F32)
          * (CROSS_HEAD_DIM ** -0.5)).astype(BF16)
    cg = jnp.dot(xb, wcg_ref[...], preferred_element_type=F32)
    outs = []
    for h in range(CROSS_HEADS):
        cols = slice(h * CROSS_HEAD_DIM, (h + 1) * CROSS_HEAD_DIM)
        mk = kv_ref[:, cols]
        mv = kv_ref[:, CROSS_WIDTH + h * CROSS_HEAD_DIM:CROSS_WIDTH + (h + 1) * CROSS_HEAD_DIM]
        s = lax.dot_general(cq[:, cols], mk, _NT, preferred_element_type=F32)
        e = jnp.exp(s - jnp.max(s, axis=-1, keepdims=True))
        p = e * (1.0 / jnp.sum(e, axis=-1, keepdims=True))
        outs.append(jnp.dot(p.astype(BF16), mv, preferred_element_type=F32))
    yc = jnp.concatenate(outs, axis=1) * _silu(cg)
    merged = merged + gate(2) * jnp.dot(yc.astype(BF16), wbc_ref[...],
                                        preferred_element_type=F32)

    out = jnp.dot(merged.astype(BF16), wo_ref[...], preferred_element_type=F32)
    r = DEEPNORM_ALPHA * x_ref[...] + out
    mu = jnp.mean(r, axis=-1, keepdims=True)
    d = r - mu
    var = jnp.mean(d * d, axis=-1, keepdims=True)
    xn = d * lax.rsqrt(var + NORM_EPS) * lng_ref[...] + lnb_ref[...]
    xo_ref[...] = xn
    xbo_ref[...] = xn.astype(BF16)


def _tail(x, xb, yf, yb, yd, kv, wz, wcq, wcg, wgl, wbs, wbd, wbc, wo, nw, gb, lng, lnb,
          seq, tm):
    T = x.shape[0]
    tm = min(tm, seq)
    tps = seq // tm
    mem_tokens = kv.shape[0] // (T // seq)
    row = lambda w: pl.BlockSpec((tm, w), lambda i: (i, 0))
    const = lambda a: pl.BlockSpec(a.shape, lambda i: (0, 0), pipeline_mode=pl.Buffered(1))
    return pl.pallas_call(
        _tail_kernel,
        grid=(T // tm,),
        in_specs=[row(D_MODEL), row(D_MODEL), row(SSD_INNER), row(SSD_INNER), row(DIFF_WIDTH),
                  pl.BlockSpec((mem_tokens, 2 * CROSS_WIDTH), lambda i: (i // tps, 0)),
                  const(wz), const(wcq), const(wcg), const(wgl), const(wbs), const(wbd),
                  const(wbc), const(wo), const(nw), const(gb), const(lng), const(lnb)],
        out_specs=[row(D_MODEL), row(D_MODEL)],
        out_shape=[jax.ShapeDtypeStruct((T, D_MODEL), F32),
                   jax.ShapeDtypeStruct((T, D_MODEL), BF16)],
        compiler_params=_cparams(("parallel",)),
        name="tail",
    )(x, xb, yf, yb, yd, kv, wz, wcq, wcg, wgl, wbs, wbd, wbc, wo, nw, gb, lng, lnb)


def _in_splits():
    sizes = [SSD_INNER, SSD_CONV_CH, 2 * SSD_HEADS, DIFF_WIDTH, DIFF_WIDTH, DIFF_WIDTH,
             DIFF_WIDTH, CROSS_WIDTH, CROSS_WIDTH, N_BRANCH * D_MODEL]
    offs = np.concatenate([[0], np.cumsum(sizes)])
    return [(int(offs[i]), int(offs[i + 1])) for i in range(len(sizes))]


def _rope_perm():
    half = DIFF_HEAD_DIM // 2
    perm = np.zeros(DIFF_WIDTH, np.int32)
    for h in range(DIFF_HEADS):
        for l in range(LANES):
            hf, c, j = l // 64, (l % 64) // half, l % half
            perm[h * LANES + l] = h * LANES + c * DIFF_HEAD_DIM + hf * half + j
    return perm


def _dt_placement():
    cols = np.zeros(2 * SSD_HEADS, np.int32)
    for d in range(2):
        for g in range(SSD_GROUPS):
            for r in range(SSD_HEADS_PER_GROUP):
                cols[d * SSD_HEADS + g * SSD_HEADS_PER_GROUP + r] = (
                    g * LANES + d * SSD_HEADS_PER_GROUP + r)
    return cols


def _per_group_lanes(v):
    out = jnp.zeros((SSD_GROUPS * LANES,), F32).at[_dt_placement()].set(v.reshape(-1).astype(F32))
    return out.reshape(SSD_GROUPS, 1, LANES)


def kernel(x, mem, positions, w_in, conv_w, conv_b, dt_bias, a_log, d_skip, ssd_norm_w,
           diff_lam, diff_norm_w, w_mem_kv, w_br_ssd, w_br_diff, w_br_cross, gate_b, w_out,
           ln_g, ln_b):
    batch, seq, _ = x.shape
    T = batch * seq
    mem_tokens = mem.shape[1]
    (sz, sxbc, sdt, sdq, sdk, sdv, sdg, scq, scg, sgl) = _in_splits()
    perm = _rope_perm()
    dt_cols = _dt_placement()

    inv = 1.0 / (ROPE_THETA ** (jnp.arange(0, DIFF_HEAD_DIM, 2, dtype=F32) / DIFF_HEAD_DIM))
    inv_t = jnp.tile(inv, LANES // inv.shape[0])[None, :]
    sgn_t = jnp.where(jnp.arange(LANES) < LANES // 2, -1.0, 1.0).astype(F32)[None, :]
    pos = positions.reshape(T, 1)

    xf = x.reshape(T, D_MODEL)
    xb = xf.astype(BF16)
    memb = mem.reshape(batch * mem_tokens, D_MODEL).astype(BF16)

    for layer in range(DEPTH):
        lambda_init = 0.8 - 0.6 * math.exp(-0.3 * layer)
        w = w_in[layer]
        wb = lambda s: w[:, s[0]:s[1]].astype(BF16)
        w_q = wb(sdq)[:, perm]
        w_k = wb(sdk)[:, perm]
        w_vt = wb(sdv).T
        w_dt = jnp.zeros((D_MODEL, SSD_GROUPS * LANES), BF16).at[:, dt_cols].set(wb(sdt))
        conv_w8 = jnp.zeros((8, SSD_CONV_CH), F32).at[:SSD_CONV].set(conv_w[layer])

        xc = _proj_conv(xb, wb(sxbc), conv_w8, conv_b[layer][None, :], seq, F32, 1024, 1024)
        q, k, vt, dg, dt = _proj_attn(xb, w_q, w_k, w_vt, wb(sdg), w_dt, pos, inv_t, sgn_t,
                                      DIFF_HEAD_DIM ** -0.5 * LOG2E, batch, seq, 1024)
        kv = _proj_plain(memb, w_mem_kv[layer].astype(BF16), BF16, 1024, 1024)

        yf, yb = _ssd(xc, dt, _per_group_lanes(dt_bias[layer]), _per_group_lanes(a_log[layer]),
                      jnp.repeat(d_skip[layer].astype(F32), SSD_HEAD_DIM)[None, :],
                      batch, seq, 2048)

        lq = diff_lam[layer].astype(F32)
        lam = (jnp.exp(jnp.sum(lq[0] * lq[1])) - jnp.exp(jnp.sum(lq[2] * lq[3]))
               + lambda_init).reshape(1, 1)
        yd = _diff_attention(lam, q, k, vt, dg, diff_norm_w[layer].astype(F32)[None, :],
                             batch, seq, 1.0 - lambda_init, 512, 512)

        xf, xb = _tail(xf, xb, yf, yb, yd, kv,
                       wb(sz), wb(scq), wb(scg), wb(sgl),
                       w_br_ssd[layer].astype(BF16), w_br_diff[layer].astype(BF16),
                       w_br_cross[layer].astype(BF16), w_out[layer].astype(BF16),
                       ssd_norm_w[layer].astype(F32)[None, :],
                       gate_b[layer].astype(F32).reshape(1, N_BRANCH * D_MODEL),
                       ln_g[layer].astype(F32)[None, :], ln_b[layer].astype(F32)[None, :],
                       seq, 256)

    return xf.reshape(batch, seq, D_MODEL).astype(x.dtype)
```

```python
import functools
import math

import numpy as np
import jax
import jax.numpy as jnp
from jax import lax
from jax.experimental import pallas as pl
from jax.experimental.pallas import tpu as pltpu

F32 = jnp.float32
BF16 = jnp.bfloat16

D_MODEL = 1024
DEPTH = 2
SSD_INNER = 2048
SSD_HEAD_DIM = 64
SSD_HEADS = 32
SSD_GROUPS = 4
SSD_HEADS_PER_GROUP = SSD_HEADS // SSD_GROUPS
SSD_STATE = 128
SSD_CONV = 5
SSD_CHUNK = 128
SSD_CONV_CH = SSD_INNER + 2 * SSD_GROUPS * SSD_STATE
DIFF_HEAD_DIM = 64
DIFF_HEADS = 8
DIFF_WIDTH = 1024
ROPE_THETA = 10000.0
CROSS_HEADS = 4
CROSS_HEAD_DIM = 256
CROSS_WIDTH = 1024
N_BRANCH = 3
DEEPNORM_ALPHA = (2 * DEPTH) ** 0.25
NORM_EPS = 1e-5

LANES = 128
HALO = 16
ROW_PHASES = 4
ONES_ROWS = 16
PAIRS_PER_TRIP = 3
SSD_STEPS_PER_TRIP = 2
LOG2E = 1.4426950408889634
VMEM_LIMIT = 56 * 1024 * 1024

PROJ_ROWS, PROJ_COLS = 1024, 1024
SSD_ROW_BLOCK = 2048
ATTN_Q_TILE, ATTN_K_BLOCK = 512, 512
TAIL_ROWS = 256

_NT = (((1,), (1,)), ((), ()))


def _sigmoid(v):
    return 1.0 / (1.0 + jnp.exp(-v))


def _silu(v):
    return v * _sigmoid(v)


def _cparams(sem):
    return pltpu.CompilerParams(dimension_semantics=sem, vmem_limit_bytes=VMEM_LIMIT)


def _proj_plain_kernel(x_ref, w_ref, o_ref):
    o_ref[...] = jnp.dot(x_ref[...], w_ref[...],
                         preferred_element_type=F32).astype(o_ref.dtype)


def _proj_plain(xb, w, out_dtype, tm, tn):
    T, K = xb.shape
    N = w.shape[1]
    tm, tn = min(tm, T), min(tn, N)
    return pl.pallas_call(
        _proj_plain_kernel,
        grid=(N // tn, T // tm),
        in_specs=[pl.BlockSpec((tm, K), lambda j, i: (i, 0)),
                  pl.BlockSpec((K, tn), lambda j, i: (0, j))],
        out_specs=pl.BlockSpec((tm, tn), lambda j, i: (i, j)),
        out_shape=jax.ShapeDtypeStruct((T, N), out_dtype),
        compiler_params=_cparams(("parallel", "parallel")),
        name="proj_plain",
    )(xb, w)


def _proj_attn_kernel(x_ref, wq_ref, wk_ref, wvt_ref, wdg_ref, wdt_ref, pos_ref, inv_ref, sgn_ref,
                      q_ref, k_ref, vt_ref, dg_ref, dt_ref, *, q_scale):
    x = x_ref[...]
    ang = pos_ref[...].astype(F32) * inv_ref[...]
    c = jnp.cos(ang)
    s = jnp.sin(ang) * sgn_ref[...]

    def rope(w_ref, o_ref, scale):
        acc = jnp.dot(x, w_ref[...], preferred_element_type=F32)
        for j in range(acc.shape[1] // LANES):
            t = acc[:, j * LANES:(j + 1) * LANES]
            o = t * c + pltpu.roll(t, LANES // 2, axis=1) * s
            if scale != 1.0:
                o = o * scale
            o_ref[:, j * LANES:(j + 1) * LANES] = o.astype(o_ref.dtype)

    rope(wq_ref, q_ref, q_scale)
    rope(wk_ref, k_ref, 1.0)
    vt_ref[...] = lax.dot_general(wvt_ref[...], x, _NT,
                                  preferred_element_type=F32).astype(vt_ref.dtype)
    dg_ref[...] = jnp.dot(x, wdg_ref[...], preferred_element_type=F32)
    dt_ref[...] = jnp.dot(x, wdt_ref[...], preferred_element_type=F32)


def _proj_attn(xb, w_q, w_k, w_vt, w_dg, w_dt, pos, inv_t, sgn_t, q_scale, batch, seq, tm):
    T, K = xb.shape
    tm = min(tm, seq)
    ns = seq // tm
    row = lambda n: pl.BlockSpec((tm, n), lambda i: (i, 0))
    const = lambda a: pl.BlockSpec(a.shape, lambda i: (0, 0), pipeline_mode=pl.Buffered(1))
    n_v, n_dt = w_vt.shape[0], w_dt.shape[1]
    return pl.pallas_call(
        functools.partial(_proj_attn_kernel, q_scale=q_scale),
        grid=(T // tm,),
        in_specs=[row(K), const(w_q), const(w_k), const(w_vt), const(w_dg), const(w_dt),
                  row(1), const(inv_t), const(sgn_t)],
        out_specs=[row(DIFF_WIDTH), row(DIFF_WIDTH),
                   pl.BlockSpec((None, n_v, tm), lambda i: (i // ns, 0, i % ns)),
                   row(DIFF_WIDTH), row(n_dt)],
        out_shape=[jax.ShapeDtypeStruct((T, DIFF_WIDTH), BF16),
                   jax.ShapeDtypeStruct((T, DIFF_WIDTH), BF16),
                   jax.ShapeDtypeStruct((batch, n_v, seq), BF16),
                   jax.ShapeDtypeStruct((T, DIFF_WIDTH), F32),
                   jax.ShapeDtypeStruct((T, n_dt), F32)],
        compiler_params=_cparams(("parallel",)),
        name="proj_attn",
    )(xb, w_q, w_k, w_vt, w_dg, w_dt, pos, inv_t, sgn_t)


def _proj_conv_kernel(xp_ref, x_ref, xn_ref, w_ref, cw_ref, cb_ref, o_ref, xcat_ref, acc_ref,
                      out_ref, *, tm, tiles_per_seq):
    i = pl.program_id(1)
    pos = i % tiles_per_seq
    xp = xp_ref[...]
    xn = xn_ref[...]
    xcat_ref[0:HALO, :] = jnp.where(pos == 0, jnp.zeros_like(xp), xp)
    xcat_ref[HALO:HALO + tm, :] = x_ref[...]
    xcat_ref[HALO + tm:, :] = jnp.where(pos == tiles_per_seq - 1, jnp.zeros_like(xn), xn)
    acc = jnp.dot(xcat_ref[...], w_ref[...], preferred_element_type=F32)
    n_slabs = acc.shape[1] // LANES
    for s in range(n_slabs):
        acc_ref[s] = acc[:, s * LANES:(s + 1) * LANES]
    pad = SSD_CONV // 2
    per_phase = tm // ROW_PHASES
    for s in range(n_slabs):
        lanes = slice(s * LANES, (s + 1) * LANES)
        for r in range(ROW_PHASES):
            out = cb_ref[:, lanes]
            for j in range(SSD_CONV):
                tap = acc_ref[s, pl.ds(HALO + r + j - pad, per_phase, stride=ROW_PHASES), :]
                out = out + cw_ref[j:j + 1, lanes] * tap
            out_ref[s, pl.ds(r, per_phase, stride=ROW_PHASES), :] = _silu(out)
        o_ref[:, lanes] = out_ref[s].astype(o_ref.dtype)


def _proj_conv(xb, w, conv_w8, conv_b, seq, out_dtype, tm, tn):
    T, K = xb.shape
    N = w.shape[1]
    tm, tn = min(tm, seq), min(tn, N)
    tps = seq // tm
    hb = tm // HALO
    nhb = T // HALO
    return pl.pallas_call(
        functools.partial(_proj_conv_kernel, tm=tm, tiles_per_seq=tps),
        grid=(N // tn, T // tm),
        in_specs=[pl.BlockSpec((HALO, K), lambda j, i: (jnp.maximum(i * hb - 1, 0), 0)),
                  pl.BlockSpec((tm, K), lambda j, i: (i, 0)),
                  pl.BlockSpec((HALO, K), lambda j, i: (jnp.minimum((i + 1) * hb, nhb - 1), 0)),
                  pl.BlockSpec((K, tn), lambda j, i: (0, j)),
                  pl.BlockSpec((8, tn), lambda j, i: (0, j)),
                  pl.BlockSpec((1, tn), lambda j, i: (0, j))],
        out_specs=pl.BlockSpec((tm, tn), lambda j, i: (i, j)),
        out_shape=jax.ShapeDtypeStruct((T, N), out_dtype),
        scratch_shapes=[pltpu.VMEM((tm + 2 * HALO, K), BF16),
                        pltpu.VMEM((tn // LANES, tm + 2 * HALO, LANES), F32),
                        pltpu.VMEM((tn // LANES, tm, LANES), F32)],
        compiler_params=_cparams(("parallel", "parallel")),
        name="proj_conv",
    )(xb, xb, xb, w, conv_w8, conv_b)


def _softplus(v):
    return jnp.maximum(v, 0.0) + jnp.log(1.0 + jnp.exp(-jnp.abs(v)))


def _split3(v):
    hi = v.astype(BF16)
    r1 = v - hi.astype(F32)
    mid = r1.astype(BF16)
    lo = (r1 - mid.astype(F32)).astype(BF16)
    return hi, mid, lo


def _ssd_keep(forward):
    qi = lax.broadcasted_iota(jnp.int32, (SSD_CHUNK, SSD_CHUNK), 0)
    si = lax.broadcasted_iota(jnp.int32, (SSD_CHUNK, SSD_CHUNK), 1)
    return si <= qi if forward else si >= qi


def _ssd_prepare(bm, cm, dt_raw, bias, a_neg, forward, pre_ref):
    tri = jnp.where(_ssd_keep(forward), 1.0, 0.0).astype(BF16)
    dtc = _softplus(dt_raw + bias)
    a = dtc * (a_neg * LOG2E)
    hi, mid, lo = _split3(a)
    cum = (jnp.dot(tri, hi, preferred_element_type=F32)
           + jnp.dot(tri, mid, preferred_element_type=F32)
           + jnp.dot(tri, lo, preferred_element_type=F32))
    pre_ref[0] = cum
    pre_ref[1] = cum.T - jnp.log2(dtc.T)
    pre_ref[2] = lax.dot_general(cm.astype(BF16), bm.astype(BF16), _NT,
                                 preferred_element_type=F32)
    pre_ref[3] = bm.T


def _ssd_main(xs, cm, pre_ref, h_ref, y_ref, row0, lane0, forward, dskip):
    Q = SSD_CHUNK
    keep = _ssd_keep(forward)
    last = Q - 1 if forward else 0
    cum = pre_ref[0]
    row_t = pre_ref[1]
    gm = pre_ref[2]
    bt = pre_ref[3]
    lane = lax.broadcasted_iota(jnp.int32, (Q, LANES), 1)
    first_half = lane < SSD_HEAD_DIM

    for pair in range(SSD_HEADS_PER_GROUP // 2):
        cols = slice(pair * LANES, (pair + 1) * LANES)
        xs_pair = xs[:, cols]
        h_pair = h_ref[:, cols]
        y_pair = None
        s_pair = None
        decs = []
        for e in range(2):
            ln = lane0 + 2 * pair + e
            hmask = first_half if e == 0 else jnp.logical_not(first_half)
            colb = jnp.broadcast_to(cum[:, ln:ln + 1], (Q, Q))
            rowb = jnp.broadcast_to(row_t[ln:ln + 1, :], (Q, Q))
            decay = jnp.where(keep, jnp.exp2(colb - rowb), 0.0)
            m1 = (gm * decay).astype(BF16)
            m2 = (cm * jnp.exp2(colb)).astype(BF16)
            lhs = jnp.concatenate([m1, m2], axis=1)
            xr = jnp.where(hmask, xs_pair, 0.0).astype(BF16)
            hr = jnp.where(hmask, h_pair, 0.0).astype(BF16)
            rhs = jnp.concatenate([xr, hr], axis=0)
            yc = jnp.dot(lhs, rhs, preferred_element_type=F32)
            y_pair = yc if y_pair is None else y_pair + yc
            tot = colb[last:last + 1, :]
            wb = (bt * jnp.exp2(tot - rowb)).astype(BF16)
            sc = jnp.dot(wb, xr, preferred_element_type=F32)
            s_pair = sc if s_pair is None else s_pair + sc
            decs.append(jnp.exp2(tot))
        dec = jnp.where(first_half[0:1, :], decs[0], decs[1])
        h_ref[:, cols] = h_pair * dec + s_pair
        if dskip is not None:
            y_pair = y_pair + dskip[:, cols] * xs_pair
        y_ref[pl.ds(row0, Q), cols] = y_pair


def _ssd_kernel(xsf_ref, bf_ref, cf_ref, dtf_ref, xsb_ref, bb_ref, cb_ref, dtb_ref,
                bias_ref, alog_ref, dsk_ref, yf_ref, yb_ref, hf_ref, hb_ref, pre_ref, *, nck):
    @pl.when(pl.program_id(2) == 0)
    def _():
        hf_ref[...] = jnp.zeros_like(hf_ref)
        hb_ref[...] = jnp.zeros_like(hb_ref)

    bias = bias_ref[...]
    a_neg = -jnp.exp(alog_ref[...])
    dsk = dsk_ref[...]
    Q = SSD_CHUNK

    def rows(j):
        if isinstance(j, int):
            return j * Q, (nck - 1 - j) * Q
        return pl.multiple_of(j * Q, Q), pl.multiple_of((nck - 1 - j) * Q, Q)

    def prepare(j, slot):
        rf, rb = rows(j)
        _ssd_prepare(bf_ref[pl.ds(rf, Q), :], cf_ref[pl.ds(rf, Q), :], dtf_ref[pl.ds(rf, Q), :],
                     bias, a_neg, True, pre_ref.at[slot, 0])
        _ssd_prepare(bb_ref[pl.ds(rb, Q), :], cb_ref[pl.ds(rb, Q), :], dtb_ref[pl.ds(rb, Q), :],
                     bias, a_neg, False, pre_ref.at[slot, 1])

    def main(j, slot):
        rf, rb = rows(j)
        _ssd_main(xsf_ref[pl.ds(rf, Q), :], cf_ref[pl.ds(rf, Q), :], pre_ref.at[slot, 0],
                  hf_ref, yf_ref, rf, 0, True, dsk)
        _ssd_main(xsb_ref[pl.ds(rb, Q), :], cb_ref[pl.ds(rb, Q), :], pre_ref.at[slot, 1],
                  hb_ref, yb_ref, rb, SSD_HEADS_PER_GROUP, False, None)

    prepare(0, 0)

    def body(t, carry):
        for u in range(SSD_STEPS_PER_TRIP):
            j = SSD_STEPS_PER_TRIP * t + u
            prepare(jnp.minimum(j + 1, nck - 1), (u + 1) % 2)
            main(j, u % 2)
        return carry

    lax.fori_loop(0, nck // SSD_STEPS_PER_TRIP, body, 0)


def _ssd(xc, dt, bias, alog, dskip, batch, seq, rb):
    T = xc.shape[0]
    rb = min(rb, seq)
    nb = seq // rb
    nck = rb // SSD_CHUNK
    assert nck % SSD_STEPS_PER_TRIP == 0 and nb * rb == seq
    gw = SSD_HEADS_PER_GROUP * SSD_HEAD_DIM
    b_off = SSD_INNER // LANES
    c_off = b_off + SSD_GROUPS

    def fwd(col):
        return lambda b, g, i: (b * nb + i, col(g))

    def bwd(col):
        return lambda b, g, i: (b * nb + nb - 1 - i, col(g))

    def specs(mk):
        return [pl.BlockSpec((rb, gw), mk(lambda g: g)),
                pl.BlockSpec((rb, LANES), mk(lambda g: b_off + g)),
                pl.BlockSpec((rb, LANES), mk(lambda g: c_off + g)),
                pl.BlockSpec((rb, LANES), mk(lambda g: g))]

    out_sds = jax.ShapeDtypeStruct((T, SSD_INNER), F32)
    return pl.pallas_call(
        functools.partial(_ssd_kernel, nck=nck),
        grid=(batch, SSD_GROUPS, nb),
        in_specs=specs(fwd) + specs(bwd) + [
            pl.BlockSpec((None, 1, LANES), lambda b, g, i: (g, 0, 0)),
            pl.BlockSpec((None, 1, LANES), lambda b, g, i: (g, 0, 0)),
            pl.BlockSpec((1, gw), lambda b, g, i: (0, g))],
        out_specs=[pl.BlockSpec((rb, gw), fwd(lambda g: g)),
                   pl.BlockSpec((rb, gw), bwd(lambda g: g))],
        out_shape=[out_sds, out_sds],
        scratch_shapes=[pltpu.VMEM((SSD_STATE, gw), F32), pltpu.VMEM((SSD_STATE, gw), F32),
                        pltpu.VMEM((2, 2, 4, SSD_CHUNK, LANES), F32)],
        compiler_params=_cparams(("parallel", "parallel", "arbitrary")),
        name="ssd_scan",
    )(xc, xc, xc, dt, xc, xc, xc, dt, bias, alog, dskip)


def _attn_kernel(lam_ref, q_ref, k_ref, vt_ref, dg_ref, nw_ref, o_ref,
                 qm_ref, s_ref, bm_ref, m_ref, acc_ref, *, tk, nk, out_scale):
    q = q_ref[...]
    lane = lax.broadcasted_iota(jnp.int32, q.shape, 1)
    map0 = (lane % DIFF_HEAD_DIM) < (DIFF_HEAD_DIM // 2)
    zero = jnp.zeros_like(q)
    qm_ref[0] = jnp.where(map0, q, zero)
    qm_ref[1] = jnp.where(map0, zero, q)
    m_ref[...] = jnp.full_like(m_ref, -jnp.inf)
    acc_ref[...] = jnp.zeros_like(acc_ref)
    ones = jnp.ones((ONES_ROWS, tk), BF16)
    dv = 2 * DIFF_HEAD_DIM

    def key_start(blk):
        return blk * tk if isinstance(blk, int) else pl.multiple_of(blk * tk, tk)

    def scores(blk, slot):
        k = k_ref[pl.ds(key_start(blk), tk), :]
        for c in range(2):
            s = lax.dot_general(k, qm_ref[c], _NT, preferred_element_type=F32)
            s_ref[slot, c] = s
            bm_ref[slot, c] = jnp.max(s, axis=0, keepdims=True)

    def softmax_pv(blk, slot):
        vt = vt_ref[:, pl.ds(key_start(blk), tk)]
        lhs = jnp.concatenate([vt, ones], axis=0)
        for c in range(2):
            s = s_ref[slot, c]
            m_old = m_ref[c]
            m_new = jnp.maximum(m_old, bm_ref[slot, c])
            alpha = jnp.exp2(m_old - m_new)
            p = jnp.exp2(s - m_new).astype(BF16)
            acc_ref[c] = acc_ref[c] * alpha + jnp.dot(lhs, p, preferred_element_type=F32)
            m_ref[c] = m_new

    scores(0, 0)

    def pair(p):
        scores(2 * p + 1, 1)
        softmax_pv(2 * p, 0)
        scores(2 * p + 2, 0)
        softmax_pv(2 * p + 1, 1)

    n_pairs = nk // 2 - 1
    trips = n_pairs // PAIRS_PER_TRIP

    def body(t, carry):
        for u in range(PAIRS_PER_TRIP):
            pair(PAIRS_PER_TRIP * t + u)
        return carry

    lax.fori_loop(0, trips, body, 0)
    for p in range(trips * PAIRS_PER_TRIP, n_pairs):
        pair(p)
    scores(nk - 1, 1)
    softmax_pv(nk - 2, 0)
    softmax_pv(nk - 1, 1)

    lam = lam_ref[0, 0]
    o = (acc_ref[0, :dv, :] * (1.0 / acc_ref[0, dv:dv + 1, :])
         - lam * (acc_ref[1, :dv, :] * (1.0 / acc_ref[1, dv:dv + 1, :])))
    ms = jnp.mean(o * o, axis=0, keepdims=True)
    o = o * lax.rsqrt(ms + NORM_EPS)
    ot = o.T * (nw_ref[...] * out_scale)
    o_ref[...] = (ot * _silu(dg_ref[...])).astype(o_ref.dtype)


def _diff_attention(lam, q, k, vt, dg, norm_w, batch, seq, out_scale, tq, tk):
    T = q.shape[0]
    tq, tk = min(tq, seq), min(tk, seq // 2)
    nq = seq // tq
    nk = seq // tk
    assert nk % 2 == 0 and nk * tk == seq
    return pl.pallas_call(
        functools.partial(_attn_kernel, tk=tk, nk=nk, out_scale=out_scale),
        grid=(batch, DIFF_HEADS, nq),
        in_specs=[pl.BlockSpec(memory_space=pltpu.SMEM),
                  pl.BlockSpec((tq, LANES), lambda b, h, i: (b * nq + i, h)),
                  pl.BlockSpec((seq, LANES), lambda b, h, i: (b, h)),
                  pl.BlockSpec((None, LANES, seq), lambda b, h, i: (b, h, 0)),
                  pl.BlockSpec((tq, LANES), lambda b, h, i: (b * nq + i, h)),
                  pl.BlockSpec((1, LANES), lambda b, h, i: (0, 0))],
        out_specs=pl.BlockSpec((tq, LANES), lambda b, h, i: (b * nq + i, h)),
        out_shape=jax.ShapeDtypeStruct((T, DIFF_WIDTH), BF16),
        scratch_shapes=[pltpu.VMEM((2, tq, LANES), BF16),
                        pltpu.VMEM((2, 2, tk, tq), F32),
                        pltpu.VMEM((2, 2, 1, tq), F32),
                        pltpu.VMEM((2, 1, tq), F32),
                        pltpu.VMEM((2, LANES + ONES_ROWS, tq), F32)],
        compiler_params=_cparams(("parallel", "parallel", "parallel")),
        name="diff_attention",
    )(lam, q, k, vt, dg, norm_w)


def _tail_kernel(x_ref, xb_ref, yf_ref, yb_ref, yd_ref, kv_ref,
                 wz_ref, wcq_ref, wcg_ref, wgl_ref, wbs_ref, wbd_ref, wbc_ref, wo_ref,
                 nw_ref, gb_ref, lng_ref, lnb_ref, xo_ref, xbo_ref):
    xb = xb_ref[...]

    def gate(idx):
        cols = slice(idx * D_MODEL, (idx + 1) * D_MODEL)
        gl = jnp.dot(xb, wgl_ref[:, cols], preferred_element_type=F32) + gb_ref[:, cols]
        return _sigmoid(gl)

    z = jnp.dot(xb, wz_ref[...], preferred_element_type=F32)
    y = (yf_ref[...] + yb_ref[...]) * _silu(z)
    y = y * lax.rsqrt(jnp.mean(y * y, axis=-1, keepdims=True) + NORM_EPS) * nw_ref[...]
    merged = gate(0) * jnp.dot(y.astype(BF16), wbs_ref[...], preferred_element_type=F32)

    merged = merged + gate(1) * jnp.dot(yd_ref[...], wbd_ref[...], preferred_element_type=F32)

    cq = (jnp.dot(xb, wcq_ref[...], preferred_element_type=F32)
          * (CROSS_HEAD_DIM ** -0.5)).astype(BF16)
    cg = jnp.dot(xb, wcg_ref[...], preferred_element_type=F32)
    outs = []
    for h in range(CROSS_HEADS):
        cols = slice(h * CROSS_HEAD_DIM, (h + 1) * CROSS_HEAD_DIM)
        mk = kv_ref[:, cols]
        mv = kv_ref[:, CROSS_WIDTH + h * CROSS_HEAD_DIM:CROSS_WIDTH + (h + 1) * CROSS_HEAD_DIM]
        s = lax.dot_general(cq[:, cols], mk, _NT, preferred_element_type=F32)
        e = jnp.exp(s - jnp.max(s, axis=-1, keepdims=True))
        p = e * (1.0 / jnp.sum(e, axis=-1, keepdims=True))
        outs.append(jnp.dot(p.astype(BF16), mv, preferred_element_type=F32))
    yc = jnp.concatenate(outs, axis=1) * _silu(cg)
    merged = merged + gate(2) * jnp.dot(yc.astype(BF16), wbc_ref[...],
                                        preferred_element_type=F32)

    out = jnp.dot(merged.astype(BF16), wo_ref[...], preferred_element_type=F32)
    r = DEEPNORM_ALPHA * x_ref[...] + out
    mu = jnp.mean(r, axis=-1, keepdims=True)
    d = r - mu
    var = jnp.mean(d * d, axis=-1, keepdims=True)
    xn = d * lax.rsqrt(var + NORM_EPS) * lng_ref[...] + lnb_ref[...]
    xo_ref[...] = xn
    xbo_ref[...] = xn.astype(BF16)


def _tail(x, xb, yf, yb, yd, kv, wz, wcq, wcg, wgl, wbs, wbd, wbc, wo, nw, gb, lng, lnb,
          seq, tm):
    T = x.shape[0]
    tm = min(tm, seq)
    tps = seq // tm
    mem_tokens = kv.shape[0] // (T // seq)
    row = lambda w: pl.BlockSpec((tm, w), lambda i: (i, 0))
    const = lambda a: pl.BlockSpec(a.shape, lambda i: (0, 0), pipeline_mode=pl.Buffered(1))
    return pl.pallas_call(
        _tail_kernel,
        grid=(T // tm,),
        in_specs=[row(D_MODEL), row(D_MODEL), row(SSD_INNER), row(SSD_INNER), row(DIFF_WIDTH),
                  pl.BlockSpec((mem_tokens, 2 * CROSS_WIDTH), lambda i: (i // tps, 0)),
                  const(wz), const(wcq), const(wcg), const(wgl), const(wbs), const(wbd),
                  const(wbc), const(wo), const(nw), const(gb), const(lng), const(lnb)],
        out_specs=[row(D_MODEL), row(D_MODEL)],
        out_shape=[jax.ShapeDtypeStruct((T, D_MODEL), F32),
                   jax.ShapeDtypeStruct((T, D_MODEL), BF16)],
        compiler_params=_cparams(("parallel",)),
        name="tail",
    )(x, xb, yf, yb, yd, kv, wz, wcq, wcg, wgl, wbs, wbd, wbc, wo, nw, gb, lng, lnb)


def _in_splits():
    sizes = [SSD_INNER, SSD_CONV_CH, 2 * SSD_HEADS, DIFF_WIDTH, DIFF_WIDTH, DIFF_WIDTH,
             DIFF_WIDTH, CROSS_WIDTH, CROSS_WIDTH, N_BRANCH * D_MODEL]
    offs = np.concatenate([[0], np.cumsum(sizes)])
    return [(int(offs[i]), int(offs[i + 1])) for i in range(len(sizes))]


def _rope_perm():
    half = DIFF_HEAD_DIM // 2
    perm = np.zeros(DIFF_WIDTH, np.int32)
    for h in range(DIFF_HEADS):
        for l in range(LANES):
            hf, c, j = l // DIFF_HEAD_DIM, (l % DIFF_HEAD_DIM) // half, l % half
            perm[h * LANES + l] = h * LANES + c * DIFF_HEAD_DIM + hf * half + j
    return perm


def _dt_placement():
    cols = np.zeros(2 * SSD_HEADS, np.int32)
    for d in range(2):
        for g in range(SSD_GROUPS):
            for r in range(SSD_HEADS_PER_GROUP):
                cols[d * SSD_HEADS + g * SSD_HEADS_PER_GROUP + r] = (
                    g * LANES + d * SSD_HEADS_PER_GROUP + r)
    return cols


def _per_group_lanes(v):
    out = jnp.zeros((SSD_GROUPS * LANES,), F32).at[_dt_placement()].set(v.reshape(-1).astype(F32))
    return out.reshape(SSD_GROUPS, 1, LANES)


def kernel(x, mem, positions, w_in, conv_w, conv_b, dt_bias, a_log, d_skip, ssd_norm_w,
           diff_lam, diff_norm_w, w_mem_kv, w_br_ssd, w_br_diff, w_br_cross, gate_b, w_out,
           ln_g, ln_b):
    batch, seq, _ = x.shape
    T = batch * seq
    mem_tokens = mem.shape[1]
    (sz, sxbc, sdt, sdq, sdk, sdv, sdg, scq, scg, sgl) = _in_splits()
    perm = _rope_perm()
    dt_cols = _dt_placement()

    inv = 1.0 / (ROPE_THETA ** (jnp.arange(0, DIFF_HEAD_DIM, 2, dtype=F32) / DIFF_HEAD_DIM))
    inv_t = jnp.tile(inv, LANES // inv.shape[0])[None, :]
    sgn_t = jnp.where(jnp.arange(LANES) < LANES // 2, -1.0, 1.0).astype(F32)[None, :]
    pos = positions.reshape(T, 1)

    xf = x.reshape(T, D_MODEL)
    xb = xf.astype(BF16)
    memb = mem.reshape(batch * mem_tokens, D_MODEL).astype(BF16)

    for layer in range(DEPTH):
        lambda_init = 0.8 - 0.6 * math.exp(-0.3 * layer)
        w = w_in[layer]
        wb = lambda s: w[:, s[0]:s[1]].astype(BF16)
        w_q = wb(sdq)[:, perm]
        w_k = wb(sdk)[:, perm]
        w_vt = wb(sdv).T
        w_dt = jnp.zeros((D_MODEL, SSD_GROUPS * LANES), BF16).at[:, dt_cols].set(wb(sdt))
        conv_w8 = jnp.zeros((8, SSD_CONV_CH), F32).at[:SSD_CONV].set(conv_w[layer])

        xc = _proj_conv(xb, wb(sxbc), conv_w8, conv_b[layer][None, :], seq, F32,
                        PROJ_ROWS, PROJ_COLS)
        q, k, vt, dg, dt = _proj_attn(xb, w_q, w_k, w_vt, wb(sdg), w_dt, pos, inv_t, sgn_t,
                                      DIFF_HEAD_DIM ** -0.5 * LOG2E, batch, seq, PROJ_ROWS)
        kv = _proj_plain(memb, w_mem_kv[layer].astype(BF16), BF16, PROJ_ROWS, PROJ_COLS)

        yf, yb = _ssd(xc, dt, _per_group_lanes(dt_bias[layer]), _per_group_lanes(a_log[layer]),
                      jnp.repeat(d_skip[layer].astype(F32), SSD_HEAD_DIM)[None, :],
                      batch, seq, SSD_ROW_BLOCK)

        lq = diff_lam[layer].astype(F32)
        lam = (jnp.exp(jnp.sum(lq[0] * lq[1])) - jnp.exp(jnp.sum(lq[2] * lq[3]))
               + lambda_init).reshape(1, 1)
        yd = _diff_attention(lam, q, k, vt, dg, diff_norm_w[layer].astype(F32)[None, :],
                             batch, seq, 1.0 - lambda_init, ATTN_Q_TILE, ATTN_K_BLOCK)

        xf, xb = _tail(xf, xb, yf, yb, yd, kv,
                       wb(sz), wb(scq), wb(scg), wb(sgl),
                       w_br_ssd[layer].astype(BF16), w_br_diff[layer].astype(BF16),
                       w_br_cross[layer].astype(BF16), w_out[layer].astype(BF16),
                       ssd_norm_w[layer].astype(F32)[None, :],
                       gate_b[layer].astype(F32).reshape(1, N_BRANCH * D_MODEL),
                       ln_g[layer].astype(F32)[None, :], ln_b[layer].astype(F32)[None, :],
                       seq, TAIL_ROWS)

    return xf.reshape(batch, seq, D_MODEL).astype(x.dtype)
```
